```python
import math
import jax, jax.numpy as jnp
from jax import lax
import numpy as np

D_MODEL = 2048
BATCH = 4
SEQ = 2048
DEPTH = 4
DEC_BATCH = 32
DEC_SEQ = 1
PAST_LEN = 16384
PAGE_SIZE = 128

N_A_LAYERS = DEPTH // 2
N_B_LAYERS = DEPTH - N_A_LAYERS
DN_HEAD_DIM = 128
DN_QK_HEADS = D_MODEL // 128
DN_V_HEADS = 2 * DN_QK_HEADS
DN_KEY_DIM = DN_QK_HEADS * DN_HEAD_DIM
DN_VAL_DIM = DN_V_HEADS * DN_HEAD_DIM
DN_CONV_DIM = 2 * DN_KEY_DIM + DN_VAL_DIM
DN_CONV_W = 4
DN_CHUNK = 64
DN_IN_DIM = DN_CONV_DIM + DN_VAL_DIM + 2 * DN_V_HEADS
SWA_HEAD_DIM = 64
SWA_HEADS = D_MODEL // SWA_HEAD_DIM
SWA_KV_HEADS = SWA_HEADS // 8
SWA_GROUP = SWA_HEADS // SWA_KV_HEADS
SWA_Q_DIM = SWA_HEADS * SWA_HEAD_DIM
SWA_KV_DIM = SWA_KV_HEADS * SWA_HEAD_DIM
WINDOW = 128
ATTN_SCALE = SWA_HEAD_DIM ** -0.5
REL_BUCKETS = 32
REL_MAX_DIST = 128
D_FF = -(-8 * D_MODEL // (3 * 256)) * 256
EPS = 1e-6

kernel_name = 'yoco_gated_deltanet_swa_sink_step'


def rms_norm(x, g):
    x32 = x.astype(jnp.float32)
    y = x32 * lax.rsqrt(jnp.mean(x32 * x32, axis=-1, keepdims=True) + EPS)
    return (y * g.astype(jnp.float32)).astype(x.dtype)


def l2_normalize(x):
    x32 = x.astype(jnp.float32)
    return x32 * lax.rsqrt(jnp.sum(x32 * x32, axis=-1, keepdims=True) + EPS)


def swiglu_ffn(h, w_gate_up, w_down):
    gate, up = jnp.split(h @ w_gate_up, 2, axis=-1)
    return (jax.nn.silu(gate) * up) @ w_down


def gated_delta_chunked(q, k, v, g, beta):
    B, L, H, dk = q.shape
    dv = v.shape[-1]
    C = DN_CHUNK
    N = L // C
    def to_chunks(t):
        return jnp.moveaxis(t.reshape((B, N, C, H) + t.shape[3:]), 3, 1)
    q, k, v, g, beta = (to_chunks(t) for t in (q, k, v, g, beta))
    gc = jnp.cumsum(g, axis=-1)
    idx = jnp.arange(C)
    incl = idx[:, None] >= idx[None, :]
    strict = idx[:, None] > idx[None, :]
    diff = gc[..., :, None] - gc[..., None, :]
    decay = jnp.where(incl, jnp.exp(jnp.where(incl, diff, 0.0)), 0.0)
    kb = k * beta[..., None]
    a_mat = jnp.where(strict, jnp.einsum('bhnid,bhnjd->bhnij', kb, k) * decay, 0.0)
    eye = jnp.eye(C, dtype=jnp.float32)
    t_mat = lax.linalg.triangular_solve(a_mat + eye, jnp.broadcast_to(eye, a_mat.shape),
                                        left_side=True, lower=True)
    u = t_mat @ (v * beta[..., None])
    w = t_mat @ (kb * jnp.exp(gc)[..., None])
    intra = jnp.einsum('bhnid,bhnjd->bhnij', q, k) * decay
    g_last = gc[..., -1]
    q_dec = q * jnp.exp(gc)[..., None]
    k_dec = k * jnp.exp(g_last[..., None] - gc)[..., None]

    def chunk_step(S, xs):
        u_n, w_n, q_n, intra_n, k_n, gl_n = xs
        v_new = u_n - w_n @ S
        o_n = q_n @ S + intra_n @ v_new
        S = S * jnp.exp(gl_n)[..., None, None] + jnp.einsum('bhcd,bhce->bhde', k_n, v_new)
        return S, o_n

    xs = tuple(jnp.moveaxis(t, 2, 0) for t in (u, w, q_dec, intra, k_dec, g_last))
    S0 = jnp.zeros((B, H, dk, dv), jnp.float32)
    S, o = lax.scan(chunk_step, S0, xs)
    o = jnp.moveaxis(o, 0, 2).reshape(B, H, L, dv).transpose(0, 2, 1, 3)
    return o, S


def gated_delta_recurrent(q, k, v, g, beta, S0):
    def token_step(S, xs):
        q_t, k_t, v_t, g_t, b_t = xs
        S = S * jnp.exp(g_t)[..., None, None]
        kv_mem = jnp.einsum('bhde,bhd->bhe', S, k_t)
        S = S + jnp.einsum('bhd,bhe->bhde', k_t, (v_t - kv_mem) * b_t[..., None])
        return S, jnp.einsum('bhde,bhd->bhe', S, q_t)
    xs = tuple(jnp.moveaxis(t, 1, 0) for t in (q, k, v, g, beta))
    S, o = lax.scan(token_step, S0, xs)
    return jnp.moveaxis(o, 0, 1), S


def deltanet_mixer(h, conv_prev, S0, w_in, conv_w, a_log, dt_bias, gnorm, w_out):
    B, L, _ = h.shape
    proj = h @ w_in
    o1 = DN_CONV_DIM
    o2 = o1 + DN_VAL_DIM
    o3 = o2 + DN_V_HEADS
    qkv_raw, z, b_raw, a_raw = proj[..., :o1], proj[..., o1:o2], proj[..., o2:o3], proj[..., o3:]
    if conv_prev is None:
        prev = jnp.zeros((B, DN_CONV_W - 1, DN_CONV_DIM), qkv_raw.dtype)
    else:
        prev = conv_prev.astype(qkv_raw.dtype)
    x_ext = jnp.concatenate([prev, qkv_raw], axis=1)
    conv = x_ext[:, 0:L] * conv_w[0]
    for j in range(1, DN_CONV_W):
        conv = conv + x_ext[:, j:j + L] * conv_w[j]
    qkv = jax.nn.silu(conv)
    rep = DN_V_HEADS // DN_QK_HEADS
    q = qkv[..., :DN_KEY_DIM].reshape(B, L, DN_QK_HEADS, DN_HEAD_DIM)
    k = qkv[..., DN_KEY_DIM:2 * DN_KEY_DIM].reshape(B, L, DN_QK_HEADS, DN_HEAD_DIM)
    v = qkv[..., 2 * DN_KEY_DIM:].reshape(B, L, DN_V_HEADS, DN_HEAD_DIM).astype(jnp.float32)
    q = jnp.repeat(l2_normalize(q) * DN_HEAD_DIM ** -0.5, rep, axis=2)
    k = jnp.repeat(l2_normalize(k), rep, axis=2)
    beta = jax.nn.sigmoid(b_raw.astype(jnp.float32))
    g = -jnp.exp(a_log.astype(jnp.float32)) * jax.nn.softplus(a_raw.astype(jnp.float32) + dt_bias.astype(jnp.float32))
    if S0 is None:
        o, S = gated_delta_chunked(q, k, v, g, beta)
    else:
        o, S = gated_delta_recurrent(q, k, v, g, beta, S0.astype(jnp.float32))
    z = z.reshape(B, L, DN_V_HEADS, DN_HEAD_DIM).astype(jnp.float32)
    o = rms_norm(o, gnorm) * jax.nn.silu(z)
    y = o.astype(h.dtype).reshape(B, L, DN_VAL_DIM) @ w_out
    return y, x_ext[:, -(DN_CONV_W - 1):], S


def t5_bucket(dist):
    d = jnp.maximum(dist, 0)
    max_exact = REL_BUCKETS // 2
    large = max_exact + (jnp.log(jnp.maximum(d, max_exact).astype(jnp.float32) / max_exact)
                         / math.log(REL_MAX_DIST / max_exact)
                         * (REL_BUCKETS - max_exact)).astype(jnp.int32)
    return jnp.where(d < max_exact, d, jnp.minimum(large, REL_BUCKETS - 1))


def sink_attention(q, k, v, dist, valid, rel_bias, sinks):
    s = jnp.einsum('...qkgd,...ckd->...kgqc', q, k, preferred_element_type=jnp.float32) * ATTN_SCALE
    bias = jnp.moveaxis(rel_bias[t5_bucket(dist)], -1, 0).reshape((SWA_KV_HEADS, SWA_GROUP) + dist.shape)
    s = jnp.where(valid, s + bias.astype(jnp.float32), -jnp.inf)
    sink = sinks.astype(jnp.float32).reshape(SWA_KV_HEADS, SWA_GROUP)[:, :, None, None]
    m = jnp.maximum(jnp.max(s, axis=-1, keepdims=True), sink)
    p = jnp.exp(s - m)
    p = p / (jnp.sum(p, axis=-1, keepdims=True) + jnp.exp(sink - m))
    return jnp.einsum('...kgqc,...ckd->...qkgd', p.astype(v.dtype), v)


def swa_prompt(q, k, v, rel_bias, sinks):
    B, L = q.shape[:2]
    nb = L // WINDOW
    qb = q.reshape(B, nb, WINDOW, SWA_KV_HEADS, SWA_GROUP, SWA_HEAD_DIM)
    pad = ((0, 0), (WINDOW, 0), (0, 0), (0, 0))
    kp = jnp.pad(k, pad).reshape(B, nb + 1, WINDOW, SWA_KV_HEADS, SWA_HEAD_DIM)
    vp = jnp.pad(v, pad).reshape(B, nb + 1, WINDOW, SWA_KV_HEADS, SWA_HEAD_DIM)
    kb = jnp.concatenate([kp[:, :-1], kp[:, 1:]], axis=2)
    vb = jnp.concatenate([vp[:, :-1], vp[:, 1:]], axis=2)
    qi = jnp.arange(WINDOW)[:, None]
    ci = jnp.arange(2 * WINDOW)[None, :]
    dist = qi + WINDOW - ci
    key_pos = jnp.arange(nb)[:, None, None] * WINDOW - WINDOW + ci
    valid = (dist >= 0) & (dist < WINDOW) & (key_pos >= 0)
    o = sink_attention(qb, kb, vb, dist, valid[:, None, None], rel_bias, sinks)
    return o.reshape(B, L, SWA_Q_DIM)


def swa_sample(q, k_all, v_all, rel_bias, sinks):
    B, T = q.shape[:2]
    Tk = k_all.shape[1]
    qi = jnp.arange(T)[:, None]
    ci = jnp.arange(Tk)[None, :]
    dist = qi + (Tk - T) - ci
    valid = (dist >= 0) & (dist < WINDOW)
    o = sink_attention(q, k_all, v_all, dist, valid, rel_bias, sinks)
    return o.reshape(B, T, SWA_Q_DIM)


def trunk(x, prompt, state_delta, state_conv, cache_k_win, cache_v_win,
          norm_mix, norm_ffn, w_in_a, conv_w_a, a_log, dt_bias, gnorm_a, w_out_a,
          norm_kv, w_kv, w_q_b, w_o_b, sinks, rel_bias, w_gate_up, w_down, norm_final):
    B, L, _ = x.shape
    new_delta, new_conv = [], []
    for i in range(N_A_LAYERS):
        h = rms_norm(x, norm_mix[i])
        y, conv_buf, S = deltanet_mixer(h, None if prompt else state_conv[i],
                                        None if prompt else state_delta[i],
                                        w_in_a[i], conv_w_a[i], a_log[i], dt_bias[i],
                                        gnorm_a[i], w_out_a[i])
        x = x + y
        x = x + swiglu_ffn(rms_norm(x, norm_ffn[i]), w_gate_up[i], w_down[i])
        new_delta.append(S)
        new_conv.append(conv_buf)
    kv = rms_norm(x, norm_kv) @ w_kv
    k_new = kv[..., :SWA_KV_DIM].reshape(B, L, SWA_KV_HEADS, SWA_HEAD_DIM)
    v_new = kv[..., SWA_KV_DIM:].reshape(B, L, SWA_KV_HEADS, SWA_HEAD_DIM)
    if prompt:
        k_all, v_all = k_new, v_new
    else:
        k_all = jnp.concatenate([cache_k_win.astype(k_new.dtype), k_new], axis=1)
        v_all = jnp.concatenate([cache_v_win.astype(v_new.dtype), v_new], axis=1)
    for j in range(N_B_LAYERS):
        i = N_A_LAYERS + j
        h = rms_norm(x, norm_mix[i])
        q = (h @ w_q_b[j]).reshape(B, L, SWA_KV_HEADS, SWA_GROUP, SWA_HEAD_DIM)
        if prompt:
            o = swa_prompt(q, k_all, v_all, rel_bias, sinks[j])
        else:
            o = swa_sample(q, k_all, v_all, rel_bias, sinks[j])
        x = x + o @ w_o_b[j]
        x = x + swiglu_ffn(rms_norm(x, norm_ffn[i]), w_gate_up[i], w_down[i])
    y = rms_norm(x, norm_final)
    return y, jnp.stack(new_delta), jnp.stack(new_conv), k_all[:, -WINDOW:], v_all[:, -WINDOW:]


def setup_inputs(seed: int = 0) -> dict:
    key = jax.random.key(seed)
    ks = jax.random.split(key, 24)
    def nrm(k, shape, scale):
        return jax.random.normal(k, shape, jnp.float32) * scale
    D = D_MODEL
    dt = jnp.exp(jax.random.uniform(ks[11], (N_A_LAYERS, DN_V_HEADS), jnp.float32,
                                    minval=math.log(1e-3), maxval=math.log(1e-1)))
    return {
        'x_prompt': nrm(ks[0], (BATCH, SEQ, D), 1.0),
        'x_sample': nrm(ks[1], (DEC_BATCH, DEC_SEQ, D), 1.0),
        'state_delta': nrm(ks[2], (N_A_LAYERS, DEC_BATCH, DN_V_HEADS, DN_HEAD_DIM, DN_HEAD_DIM), 0.05),
        'state_conv': nrm(ks[3], (N_A_LAYERS, DEC_BATCH, DN_CONV_W - 1, DN_CONV_DIM), 1.0),
        'cache_k_win': nrm(ks[4], (DEC_BATCH, WINDOW, SWA_KV_HEADS, SWA_HEAD_DIM), 1.0),
        'cache_v_win': nrm(ks[5], (DEC_BATCH, WINDOW, SWA_KV_HEADS, SWA_HEAD_DIM), 1.0),
        'norm_mix': 1.0 + nrm(ks[6], (DEPTH, D), 0.05),
        'norm_ffn': 1.0 + nrm(ks[7], (DEPTH, D), 0.05),
        'w_in_a': nrm(ks[8], (N_A_LAYERS, D, DN_IN_DIM), D ** -0.5),
        'conv_w_a': nrm(ks[9], (N_A_LAYERS, DN_CONV_W, DN_CONV_DIM), DN_CONV_W ** -0.5),
        'a_log': jnp.log(jax.random.uniform(ks[10], (N_A_LAYERS, DN_V_HEADS), jnp.float32, minval=1.0, maxval=16.0)),
        'dt_bias': dt + jnp.log(-jnp.expm1(-dt)),
        'gnorm_a': 1.0 + nrm(ks[12], (N_A_LAYERS, DN_HEAD_DIM), 0.05),
        'w_out_a': nrm(ks[13], (N_A_LAYERS, DN_VAL_DIM, D), DN_VAL_DIM ** -0.5),
        'norm_kv': 1.0 + nrm(ks[14], (D,), 0.05),
        'w_kv': nrm(ks[15], (D, 2 * SWA_KV_DIM), D ** -0.5),
        'w_q_b': nrm(ks[16], (N_B_LAYERS, D, SWA_Q_DIM), D ** -0.5),
        'w_o_b': nrm(ks[17], (N_B_LAYERS, SWA_Q_DIM, D), SWA_Q_DIM ** -0.5),
        'sinks': nrm(ks[18], (N_B_LAYERS, SWA_HEADS), 0.5),
        'rel_bias': nrm(ks[19], (REL_BUCKETS, SWA_HEADS), 0.5),
        'w_gate_up': nrm(ks[20], (DEPTH, D, 2 * D_FF), D ** -0.5),
        'w_down': nrm(ks[21], (DEPTH, D_FF, D), D_FF ** -0.5),
        'norm_final': 1.0 + nrm(ks[22], (D,), 0.05),
    }


def reference(x_prompt, x_sample, state_delta, state_conv, cache_k_win, cache_v_win,
              norm_mix, norm_ffn, w_in_a, conv_w_a, a_log, dt_bias, gnorm_a, w_out_a,
              norm_kv, w_kv, w_q_b, w_o_b, sinks, rel_bias, w_gate_up, w_down, norm_final):
    y_prompt, delta_p, conv_p, k_win_p, v_win_p = trunk(
        x_prompt, True, None, None, None, None,
        norm_mix, norm_ffn, w_in_a, conv_w_a, a_log, dt_bias, gnorm_a, w_out_a,
        norm_kv, w_kv, w_q_b, w_o_b, sinks, rel_bias, w_gate_up, w_down, norm_final)
    y_sample, delta_s, conv_s, k_win_s, v_win_s = trunk(
        x_sample, False, state_delta, state_conv, cache_k_win, cache_v_win,
        norm_mix, norm_ffn, w_in_a, conv_w_a, a_log, dt_bias, gnorm_a, w_out_a,
        norm_kv, w_kv, w_q_b, w_o_b, sinks, rel_bias, w_gate_up, w_down, norm_final)
    return (y_prompt, y_sample, delta_p, conv_p, k_win_p, v_win_p, delta_s, conv_s, k_win_s, v_win_s)
```

```python
import functools
import math

import numpy as np
import jax
import jax.numpy as jnp
from jax import lax
from jax.experimental import pallas as pl
from jax.experimental.pallas import tpu as pltpu

F32 = jnp.float32
BF16 = jnp.bfloat16

D_MODEL = 2048
N_A_LAYERS = 2
N_B_LAYERS = 2
DN_HEAD_DIM = 128
DN_QK_HEADS = 16
DN_V_HEADS = 32
DN_KEY_DIM = DN_QK_HEADS * DN_HEAD_DIM
DN_VAL_DIM = DN_V_HEADS * DN_HEAD_DIM
DN_CONV_DIM = 2 * DN_KEY_DIM + DN_VAL_DIM
DN_CONV_W = 4
DN_CHUNK = 64
DN_QKVZ_DIM = DN_CONV_DIM + DN_VAL_DIM
SWA_HEAD_DIM = 64
SWA_HEADS = 32
SWA_KV_HEADS = 4
SWA_GROUP = SWA_HEADS // SWA_KV_HEADS
SWA_KV_DIM = SWA_KV_HEADS * SWA_HEAD_DIM
WINDOW = 128
ATTN_SCALE = SWA_HEAD_DIM ** -0.5
REL_BUCKETS = 32
REL_MAX_DIST = 128
D_FF = 5632
EPS = 1e-6

V7X_VMEM_LIMIT_BYTES = 56 * 1024 * 1024
ROW_TILE = 512
COL_TILE = 512


def _cparams(n_axes):
    return pltpu.CompilerParams(
        dimension_semantics=("arbitrary",) * n_axes,
        vmem_limit_bytes=V7X_VMEM_LIMIT_BYTES)


def _sigmoid(x):
    return 1.0 / (1.0 + jnp.exp(-x))


def _silu(x):
    return x * _sigmoid(x)


def _softplus(x):
    return jnp.maximum(x, 0.0) + jnp.log1p(jnp.exp(-jnp.abs(x)))


def _dot(a, b):
    return jnp.dot(a.astype(BF16), b.astype(BF16), preferred_element_type=F32)


def _dot_nt(a, b):
    return lax.dot_general(a.astype(BF16), b.astype(BF16), (((1,), (1,)), ((), ())),
                           preferred_element_type=F32)


def _dot_tn(a, b):
    return lax.dot_general(a.astype(BF16), b.astype(BF16), (((0,), (0,)), ((), ())),
                           preferred_element_type=F32)


def _split3(x):
    x1 = x.astype(BF16)
    r1 = x - x1.astype(F32)
    x2 = r1.astype(BF16)
    r2 = r1 - x2.astype(F32)
    return x1, x2, r2.astype(BF16)


def _rms(x, g):
    ms = jnp.mean(x * x, axis=-1, keepdims=True)
    return x * lax.rsqrt(ms + EPS) * g


def _rmsnorm_kernel(xm_ref, xs_ref, g_ref, om_ref, os_ref, *, n_main):
    m = pl.program_id(0)

    @pl.when(m < n_main)
    def _():
        om_ref[...] = _rms(xm_ref[...], g_ref[...]).astype(om_ref.dtype)

    @pl.when(m == n_main)
    def _():
        os_ref[...] = _rms(xs_ref[...], g_ref[...]).astype(os_ref.dtype)


def _rmsnorm_pair(xm, xs, g, out_dtype):
    mm, d = xm.shape
    ms = xs.shape[0]
    n_main = mm // ROW_TILE
    main_idx = lambda m: (jnp.minimum(m, n_main - 1), 0)
    return pl.pallas_call(
        functools.partial(_rmsnorm_kernel, n_main=n_main),
        grid=(n_main + 1,),
        in_specs=[pl.BlockSpec((ROW_TILE, d), main_idx),
                  pl.BlockSpec((ms, d), lambda m: (0, 0)),
                  pl.BlockSpec((1, d), lambda m: (0, 0))],
        out_specs=[pl.BlockSpec((ROW_TILE, d), main_idx),
                   pl.BlockSpec((ms, d), lambda m: (0, 0))],
        out_shape=[jax.ShapeDtypeStruct((mm, d), out_dtype),
                   jax.ShapeDtypeStruct((ms, d), out_dtype)],
        compiler_params=_cparams(1),
        name="rmsnorm",
    )(xm, xs, g.reshape(1, d))


def _linear_kernel(*refs, n_main, has_res, swiglu):
    it = iter(refs)
    xm_ref, xs_ref = next(it), next(it)
    w_ref = next(it)
    wu_ref = next(it) if swiglu else None
    rm_ref = next(it) if has_res else None
    rs_ref = next(it) if has_res else None
    om_ref, os_ref = next(it), next(it)
    wb_ref = next(it)
    wub_ref = next(it) if swiglu else None
    m = pl.program_id(1)

    @pl.when(m == 0)
    def _():
        wb_ref[...] = w_ref[...].astype(BF16)
        if swiglu:
            wub_ref[...] = wu_ref[...].astype(BF16)

    def compute(x_ref, r_ref, o_ref):
        x = x_ref[...]
        acc = jnp.dot(x, wb_ref[...], preferred_element_type=F32)
        if swiglu:
            acc = _silu(acc) * jnp.dot(x, wub_ref[...], preferred_element_type=F32)
        if has_res:
            acc = r_ref[...] + acc
        o_ref[...] = acc.astype(o_ref.dtype)

    @pl.when(m < n_main)
    def _():
        compute(xm_ref, rm_ref, om_ref)

    @pl.when(m == n_main)
    def _():
        compute(xs_ref, rs_ref, os_ref)


def _linear_pair(xm, xs, w3, layer, col0, n_cols, *, tn=COL_TILE, res=None,
                 swiglu_up_col0=None, out_dtype=F32):
    mm, k = xm.shape
    ms = xs.shape[0]
    n_main = mm // ROW_TILE
    nn = n_cols // tn
    assert n_cols % tn == 0 and col0 % tn == 0
    swiglu = swiglu_up_col0 is not None
    has_res = res is not None
    cb = col0 // tn
    main_x = lambda n, m: (jnp.minimum(m, n_main - 1), 0)
    main_o = lambda n, m: (jnp.minimum(m, n_main - 1), n)
    in_specs = [pl.BlockSpec((ROW_TILE, k), main_x),
                pl.BlockSpec((ms, k), lambda n, m: (0, 0)),
                pl.BlockSpec((None, k, tn), lambda n, m: (layer, 0, cb + n))]
    args = [xm, xs, w3]
    scratch = [pltpu.VMEM((k, tn), BF16)]
    if swiglu:
        ub = swiglu_up_col0 // tn
        assert swiglu_up_col0 % tn == 0
        in_specs.append(pl.BlockSpec((None, k, tn), lambda n, m: (layer, 0, ub + n)))
        args.append(w3)
        scratch.append(pltpu.VMEM((k, tn), BF16))
    if has_res:
        in_specs += [pl.BlockSpec((ROW_TILE, tn), main_o),
                     pl.BlockSpec((ms, tn), lambda n, m: (0, n))]
        args += [res[0], res[1]]
    return pl.pallas_call(
        functools.partial(_linear_kernel, n_main=n_main, has_res=has_res, swiglu=swiglu),
        grid=(nn, n_main + 1),
        in_specs=in_specs,
        out_specs=[pl.BlockSpec((ROW_TILE, tn), main_o),
                   pl.BlockSpec((ms, tn), lambda n, m: (0, n))],
        out_shape=[jax.ShapeDtypeStruct((mm, n_cols), out_dtype),
                   jax.ShapeDtypeStruct((ms, n_cols), out_dtype)],
        scratch_shapes=scratch,
        compiler_params=_cparams(2),
        name="linear",
    )(*args)


def _conv_silu_chunk(x_ref, e_ref, w_ref):
    c = DN_CHUNK
    e_ref[8:8 + c, :] = x_ref[...]
    w = w_ref[...]
    acc = e_ref[5:5 + c, :] * w[0:1]
    acc = acc + e_ref[6:6 + c, :] * w[1:2]
    acc = acc + e_ref[7:7 + c, :] * w[2:3]
    acc = acc + e_ref[8:8 + c, :] * w[3:4]
    e_ref[0:8, :] = e_ref[c:c + 8, :]
    return _silu(acc)


def _l2n(x):
    return x * lax.rsqrt(jnp.sum(x * x, axis=-1, keepdims=True) + EPS)


def _delta_prompt_kernel(q_ref, k_ref, v_ref, z_ref, ba_ref, cwq_ref, cwk_ref, cwv_ref,
                         alog_ref, dtb_ref, gn_ref,
                         o_ref, s_ref,
                         eq_ref, ek_ref, ev_ref, qs_ref, ks_ref, vs_ref, gcb_ref, bb_ref,
                         gr_ref, os_ref):
    c = DN_CHUNK
    hd = DN_HEAD_DIM
    n = pl.program_id(1)

    @pl.when(n == 0)
    def _():
        s_ref[...] = jnp.zeros_like(s_ref)
        eq_ref[0:8, :] = jnp.zeros((8, DN_KEY_DIM), F32)
        ek_ref[0:8, :] = jnp.zeros((8, DN_KEY_DIM), F32)
        ev_ref[0:8, :] = jnp.zeros((8, DN_VAL_DIM), F32)

    qc = _conv_silu_chunk(q_ref, eq_ref, cwq_ref)
    kc = _conv_silu_chunk(k_ref, ek_ref, cwk_ref)
    vc = _conv_silu_chunk(v_ref, ev_ref, cwv_ref)
    for h in range(DN_QK_HEADS):
        sl = slice(h * hd, (h + 1) * hd)
        qs_ref[h] = _l2n(qc[:, sl]) * (hd ** -0.5)
        ks_ref[h] = _l2n(kc[:, sl])
        vs_ref[h, 0:c, :] = vc[:, (2 * h) * hd:(2 * h + 1) * hd]
        vs_ref[h, c:2 * c, :] = vc[:, (2 * h + 1) * hd:(2 * h + 2) * hd]

    ba = ba_ref[...]
    beta = _sigmoid(ba[:, 0:DN_V_HEADS])
    g = -jnp.exp(alog_ref[...]) * _softplus(ba[:, DN_V_HEADS:2 * DN_V_HEADS] + dtb_ref[...])
    r64 = lax.broadcasted_iota(jnp.int32, (c, c), 0)
    c64 = lax.broadcasted_iota(jnp.int32, (c, c), 1)
    tri = (r64 >= c64).astype(BF16)
    g1, g2, g3 = _split3(g)
    gc = (jnp.dot(tri, g1, preferred_element_type=F32)
          + jnp.dot(tri, g2, preferred_element_type=F32)
          + jnp.dot(tri, g3, preferred_element_type=F32))
    pr = lax.broadcasted_iota(jnp.int32, (DN_QK_HEADS, DN_V_HEADS), 0)
    pc = lax.broadcasted_iota(jnp.int32, (DN_QK_HEADS, DN_V_HEADS), 1)
    sel_even = (pc == 2 * pr).astype(BF16)
    sel_odd = (pc == 2 * pr + 1).astype(BF16)
    nt = (((1,), (1,)), ((), ()))
    zpad = jnp.zeros((c, DN_V_HEADS), BF16)
    gr = jnp.zeros((DN_QK_HEADS, 2 * c), F32)
    for part in _split3(gc):
        gr = gr + lax.dot_general(sel_even, jnp.concatenate([part, zpad], axis=0), nt,
                                  preferred_element_type=F32)
        gr = gr + lax.dot_general(sel_odd, jnp.concatenate([zpad, part], axis=0), nt,
                                  preferred_element_type=F32)
    gr_ref[...] = gr
    for h in range(DN_V_HEADS):
        p, half = h // 2, h % 2
        gcb_ref[p, half * c:(half + 1) * c, :] = jnp.broadcast_to(gc[:, h:h + 1], (c, hd))
        bb_ref[p, half * c:(half + 1) * c, :] = jnp.broadcast_to(beta[:, h:h + 1], (c, hd))

    r2 = lax.broadcasted_iota(jnp.int32, (2 * c, 2 * c), 0)
    c2 = lax.broadcasted_iota(jnp.int32, (2 * c, 2 * c), 1)
    same = (r2 >= c) == (c2 >= c)
    incl = same & (r2 >= c2)
    strict = same & (r2 > c2)
    eye = (r2 == c2).astype(F32)

    def pair_body(p, carry):
        q = qs_ref[p]
        k = ks_ref[p]
        v2 = vs_ref[p]
        gcb = gcb_ref[p]
        bb = bb_ref[p]
        grow = gr_ref[pl.ds(p, 1), :]
        k2 = jnp.concatenate([k, k], axis=0)
        q2 = jnp.concatenate([q, q], axis=0)
        diff = gcb - grow
        decay = jnp.where(incl, jnp.exp(jnp.where(incl, diff, 0.0)), 0.0)
        kk = _dot_nt(k2, k2)
        a_mat = jnp.where(strict, bb * kk * decay, 0.0)
        t_mat = eye - a_mat
        pw = _dot(a_mat, a_mat)
        for _ in range(4):
            t_mat = t_mat + _dot(t_mat, pw)
            pw = _dot(pw, pw)
        t_mat = t_mat + _dot(t_mat, pw)
        egc = jnp.exp(gcb)
        u2 = _dot(t_mat, v2 * bb)
        w2 = _dot(t_mat, k2 * (bb * egc))
        intra = _dot_nt(q2, k2) * decay
        q_dec = q2 * egc
        gl_a = gcb[c - 1:c, :]
        gl_b = gcb[2 * c - 1:2 * c, :]
        gl2 = jnp.concatenate([jnp.broadcast_to(gl_a, (c, hd)), jnp.broadcast_to(gl_b, (c, hd))], axis=0)
        k_dec = k2 * jnp.exp(gl2 - gcb)
        s_a = s_ref[2 * p]
        s_b = s_ref[2 * p + 1]
        ws = jnp.concatenate([_dot(w2[0:c], s_a), _dot(w2[c:2 * c], s_b)], axis=0)
        v_new = u2 - ws
        qs = jnp.concatenate([_dot(q_dec[0:c], s_a), _dot(q_dec[c:2 * c], s_b)], axis=0)
        os_ref[p] = qs + _dot(intra, v_new)
        s_ref[2 * p] = s_a * jnp.exp(gl_a) + _dot_tn(k_dec[0:c], v_new[0:c])
        s_ref[2 * p + 1] = s_b * jnp.exp(gl_b) + _dot_tn(k_dec[c:2 * c], v_new[c:2 * c])
        return carry

    lax.fori_loop(0, DN_QK_HEADS, pair_body, 0)

    gn = gn_ref[...]
    for h in range(DN_V_HEADS):
        p, half = h // 2, h % 2
        o = os_ref[p, half * c:(half + 1) * c, :]
        zz = z_ref[:, h * hd:(h + 1) * hd]
        o_ref[:, h * hd:(h + 1) * hd] = (_rms(o, gn) * _silu(zz)).astype(o_ref.dtype)


def _delta_prompt(proj, ba, conv_w, layer, a_log, dt_bias, gnorm, batch, seq):
    c = DN_CHUNK
    nchunk = seq // c
    row = lambda b, n: b * nchunk + n
    kd, vd, hd = DN_KEY_DIM, DN_VAL_DIM, DN_HEAD_DIM
    in_specs = [
        pl.BlockSpec((c, kd), lambda b, n: (row(b, n), 0)),
        pl.BlockSpec((c, kd), lambda b, n: (row(b, n), 1)),
        pl.BlockSpec((c, vd), lambda b, n: (row(b, n), 1)),
        pl.BlockSpec((c, vd), lambda b, n: (row(b, n), 2)),
        pl.BlockSpec((c, 2 * DN_V_HEADS), lambda b, n: (row(b, n), 0)),
        pl.BlockSpec((None, DN_CONV_W, kd), lambda b, n: (layer, 0, 0)),
        pl.BlockSpec((None, DN_CONV_W, kd), lambda b, n: (layer, 0, 1)),
        pl.BlockSpec((None, DN_CONV_W, vd), lambda b, n: (layer, 0, 1)),
        pl.BlockSpec((None, 1, DN_V_HEADS), lambda b, n: (layer, 0, 0)),
        pl.BlockSpec((None, 1, DN_V_HEADS), lambda b, n: (layer, 0, 0)),
        pl.BlockSpec((None, 1, hd), lambda b, n: (layer, 0, 0)),
    ]
    out_specs = [
        pl.BlockSpec((c, vd), lambda b, n: (row(b, n), 0)),
        pl.BlockSpec((None, DN_V_HEADS, hd, hd), lambda b, n: (b, 0, 0, 0)),
    ]
    scratch = [
        pltpu.VMEM((c + 8, kd), F32), pltpu.VMEM((c + 8, kd), F32), pltpu.VMEM((c + 8, vd), F32),
        pltpu.VMEM((DN_QK_HEADS, c, hd), F32), pltpu.VMEM((DN_QK_HEADS, c, hd), F32),
        pltpu.VMEM((DN_QK_HEADS, 2 * c, hd), F32),
        pltpu.VMEM((DN_QK_HEADS, 2 * c, hd), F32), pltpu.VMEM((DN_QK_HEADS, 2 * c, hd), F32),
        pltpu.VMEM((DN_QK_HEADS, 2 * c), F32),
        pltpu.VMEM((DN_QK_HEADS, 2 * c, hd), F32),
    ]
    return pl.pallas_call(
        _delta_prompt_kernel,
        grid=(batch, nchunk),
        in_specs=in_specs,
        out_specs=out_specs,
        out_shape=[jax.ShapeDtypeStruct((batch * seq, vd), BF16),
                   jax.ShapeDtypeStruct((batch, DN_V_HEADS, hd, hd), F32)],
        scratch_shapes=scratch,
        compiler_params=_cparams(2),
        name="delta_prompt",
    )(proj, proj, proj, proj, ba, conv_w, conv_w, conv_w, a_log, dt_bias, gnorm)


def _delta_sample_kernel(q_ref, k_ref, v_ref, z_ref, ba_ref, pq_ref, pk_ref, pv_ref,
                         cwq_ref, cwk_ref, cwv_ref, alog_ref, dtb_ref, gn_ref, s0_ref,
                         o_ref, s_ref, kt_ref, qt_ref, osc_ref):
    hd = DN_HEAD_DIM
    nb = q_ref.shape[0]
    p = pl.program_id(0)

    def conv(prev_ref, x_ref, w_ref):
        w = w_ref[...]
        acc = prev_ref[:, 0, :] * w[0:1]
        acc = acc + prev_ref[:, 1, :] * w[1:2]
        acc = acc + prev_ref[:, 2, :] * w[2:3]
        acc = acc + x_ref[...] * w[3:4]
        return _silu(acc)

    q = _l2n(conv(pq_ref, q_ref, cwq_ref)) * (hd ** -0.5)
    k = _l2n(conv(pk_ref, k_ref, cwk_ref))
    v2 = conv(pv_ref, v_ref, cwv_ref)

    ba = ba_ref[...]
    beta_all = _sigmoid(ba[:, 0:DN_V_HEADS])
    g_all = -jnp.exp(alog_ref[...]) * _softplus(ba[:, DN_V_HEADS:2 * DN_V_HEADS] + dtb_ref[...])
    lane = lax.broadcasted_iota(jnp.int32, (nb, DN_V_HEADS), 1)

    def col(x, h):
        return jnp.sum(jnp.where(lane == h, x, 0.0), axis=1, keepdims=True)

    ri = lax.broadcasted_iota(jnp.int32, (hd, hd), 0)
    ci = lax.broadcasted_iota(jnp.int32, (hd, hd), 1)
    eye = (ri == ci).astype(BF16)

    def transpose(x):
        return sum(lax.dot_general(eye, part, (((1,), (1,)), ((), ())), preferred_element_type=F32)
                   for part in _split3(x))

    kt_ref[...] = transpose(k)
    qt_ref[...] = transpose(q)
    eg = [jnp.exp(col(g_all, 2 * p + hh)) for hh in range(2)]
    bt = [col(beta_all, 2 * p + hh) for hh in range(2)]

    for b in range(nb):
        kcb = jnp.broadcast_to(kt_ref[:, b:b + 1], (hd, hd))
        qcb = jnp.broadcast_to(qt_ref[:, b:b + 1], (hd, hd))
        for hh in range(2):
            s = s0_ref[b, hh] * eg[hh][b:b + 1, :]
            kv_mem = jnp.sum(s * kcb, axis=0, keepdims=True)
            delta = (v2[b:b + 1, hh * hd:(hh + 1) * hd] - kv_mem) * bt[hh][b:b + 1, :]
            s = s + kcb * delta
            s_ref[b, hh] = s
            osc_ref[hh, b:b + 1, :] = jnp.sum(s * qcb, axis=0, keepdims=True)

    gn = gn_ref[...]
    for hh in range(2):
        zz = z_ref[:, hh * hd:(hh + 1) * hd]
        o_ref[:, hh * hd:(hh + 1) * hd] = (_rms(osc_ref[hh], gn) * _silu(zz)).astype(o_ref.dtype)


def _delta_sample(proj_s, ba_s, state_conv, state_delta, conv_w, layer, a_log, dt_bias, gnorm):
    nb = proj_s.shape[0]
    hd = DN_HEAD_DIM
    npairs = DN_QK_HEADS
    kb = DN_KEY_DIM // hd
    vb = DN_CONV_DIM // (2 * hd) - DN_VAL_DIM // (2 * hd)
    zb = DN_CONV_DIM // (2 * hd)
    in_specs = [
        pl.BlockSpec((nb, hd), lambda p: (0, p)),
        pl.BlockSpec((nb, hd), lambda p: (0, kb + p)),
        pl.BlockSpec((nb, 2 * hd), lambda p: (0, vb + p)),
        pl.BlockSpec((nb, 2 * hd), lambda p: (0, zb + p)),
        pl.BlockSpec((nb, 2 * DN_V_HEADS), lambda p: (0, 0)),
        pl.BlockSpec((None, nb, DN_CONV_W - 1, hd), lambda p: (layer, 0, 0, p)),
        pl.BlockSpec((None, nb, DN_CONV_W - 1, hd), lambda p: (layer, 0, 0, kb + p)),
        pl.BlockSpec((None, nb, DN_CONV_W - 1, 2 * hd), lambda p: (layer, 0, 0, vb + p)),
        pl.BlockSpec((None, DN_CONV_W, hd), lambda p: (layer, 0, p)),
        pl.BlockSpec((None, DN_CONV_W, hd), lambda p: (layer, 0, kb + p)),
        pl.BlockSpec((None, DN_CONV_W, 2 * hd), lambda p: (layer, 0, vb + p)),
        pl.BlockSpec((None, 1, DN_V_HEADS), lambda p: (layer, 0, 0)),
        pl.BlockSpec((None, 1, DN_V_HEADS), lambda p: (layer, 0, 0)),
        pl.BlockSpec((None, 1, hd), lambda p: (layer, 0, 0)),
        pl.BlockSpec((None, nb, 2, hd, hd), lambda p: (layer, 0, p, 0, 0)),
    ]
    out_specs = [
        pl.BlockSpec((nb, 2 * hd), lambda p: (0, p)),
        pl.BlockSpec((nb, 2, hd, hd), lambda p: (0, p, 0, 0)),
    ]
    return pl.pallas_call(
        _delta_sample_kernel,
        grid=(npairs,),
        in_specs=in_specs,
        out_specs=out_specs,
        out_shape=[jax.ShapeDtypeStruct((nb, DN_VAL_DIM), BF16),
                   jax.ShapeDtypeStruct((nb, DN_V_HEADS, hd, hd), F32)],
        scratch_shapes=[pltpu.VMEM((hd, nb), F32), pltpu.VMEM((hd, nb), F32),
                        pltpu.VMEM((2, nb, hd), F32)],
        compiler_params=_cparams(1),
        name="delta_sample",
    )(proj_s, proj_s, proj_s, proj_s, ba_s, state_conv, state_conv, state_conv,
      conv_w, conv_w, conv_w, a_log, dt_bias, gnorm, state_delta)


def _bucket_thresholds():
    d = np.arange(0, REL_MAX_DIST, dtype=np.int64)
    max_exact = REL_BUCKETS // 2
    large = max_exact + (np.log(np.maximum(d, max_exact).astype(np.float32) / max_exact)
                         / math.log(REL_MAX_DIST / max_exact)
                         * (REL_BUCKETS - max_exact)).astype(np.int32)
    bucket = np.where(d < max_exact, d, np.minimum(large, REL_BUCKETS - 1))
    assert np.all(np.diff(bucket) >= 0)
    return [int(np.argmax(bucket >= t)) if np.any(bucket >= t) else None for t in range(REL_BUCKETS)]


def _relbias_kernel(rb_ref, o_ref):
    h = pl.program_id(0)
    qi = lax.broadcasted_iota(jnp.int32, (WINDOW, 2 * WINDOW), 0)
    ci = lax.broadcasted_iota(jnp.int32, (WINDOW, 2 * WINDOW), 1)
    dist = qi + WINDOW - ci
    val = jnp.full((WINDOW, 2 * WINDOW), rb_ref[0, h], F32)
    for t, thr in enumerate(_bucket_thresholds()):
        if t > 0 and thr is not None:
            val = jnp.where(dist >= thr, rb_ref[t, h], val)
    o_ref[...] = val


def _relbias(rel_bias):
    return pl.pallas_call(
        _relbias_kernel,
        grid=(SWA_HEADS,),
        in_specs=[pl.BlockSpec(memory_space=pltpu.SMEM)],
        out_specs=pl.BlockSpec((None, WINDOW, 2 * WINDOW), lambda h: (h, 0, 0)),
        out_shape=jax.ShapeDtypeStruct((SWA_HEADS, WINDOW, 2 * WINDOW), F32),
        compiler_params=_cparams(1),
        name="relbias",
    )(rel_bias)


def _attn_prompt_kernel(q_ref, kvc_ref, kvp_ref, bias_ref, sink_ref, o_ref):
    w = WINDOW
    hd = SWA_HEAD_DIM
    n = pl.program_id(1)
    qi = lax.broadcasted_iota(jnp.int32, (w, w), 0)
    ci = lax.broadcasted_iota(jnp.int32, (w, w), 1)
    valid_p = (ci > qi) & (n > 0)
    valid_c = ci <= qi
    neg = -jnp.inf
    for h in range(SWA_HEADS):
        kvh = h // SWA_GROUP
        q = q_ref[:, h * hd:(h + 1) * hd]
        kc = kvc_ref[:, kvh * hd:(kvh + 1) * hd]
        kp = kvp_ref[:, kvh * hd:(kvh + 1) * hd]
        vc = kvc_ref[:, SWA_KV_DIM + kvh * hd:SWA_KV_DIM + (kvh + 1) * hd]
        vp = kvp_ref[:, SWA_KV_DIM + kvh * hd:SWA_KV_DIM + (kvh + 1) * hd]
        s_p = jnp.where(valid_p, _dot_nt(q, kp) * ATTN_SCALE + bias_ref[h, :, 0:w], neg)
        s_c = jnp.where(valid_c, _dot_nt(q, kc) * ATTN_SCALE + bias_ref[h, :, w:2 * w], neg)
        sink = sink_ref[0, h]
        mx = jnp.maximum(jnp.maximum(jnp.max(s_p, axis=-1, keepdims=True),
                                     jnp.max(s_c, axis=-1, keepdims=True)), sink)
        p_p = jnp.exp(s_p - mx)
        p_c = jnp.exp(s_c - mx)
        den = (jnp.sum(p_p, axis=-1, keepdims=True) + jnp.sum(p_c, axis=-1, keepdims=True)
               + jnp.exp(sink - mx))
        o = (_dot(p_p, vp) + _dot(p_c, vc)) / den
        o_ref[:, h * hd:(h + 1) * hd] = o.astype(o_ref.dtype)


def _attn_prompt(qm, kvm, bias, sinks_row, batch, seq):
    w = WINDOW
    nb = seq // w
    qd = SWA_HEADS * SWA_HEAD_DIM
    return pl.pallas_call(
        _attn_prompt_kernel,
        grid=(batch, nb),
        in_specs=[pl.BlockSpec((w, qd), lambda b, n: (b * nb + n, 0)),
                  pl.BlockSpec((w, 2 * SWA_KV_DIM), lambda b, n: (b * nb + n, 0)),
                  pl.BlockSpec((w, 2 * SWA_KV_DIM), lambda b, n: (b * nb + jnp.maximum(n - 1, 0), 0)),
                  pl.BlockSpec((SWA_HEADS, w, 2 * w), lambda b, n: (0, 0, 0)),
                  pl.BlockSpec(memory_space=pltpu.SMEM)],
        out_specs=pl.BlockSpec((w, qd), lambda b, n: (b * nb + n, 0)),
        out_shape=jax.ShapeDtypeStruct((batch * seq, qd), BF16),
        compiler_params=_cparams(2),
        name="attn_prompt",
    )(qm, kvm, kvm, bias, sinks_row)


def _attn_sample_kernel(q_ref, kn_ref, ck_ref, cv_ref, bias_ref, sink_ref, o_ref):
    w = WINDOW
    hd = SWA_HEAD_DIM
    g = SWA_GROUP
    b = pl.program_id(0)
    lane = lax.broadcasted_iota(jnp.int32, (g, w), 1)
    for kvh in range(SWA_KV_HEADS):
        q = q_ref[kvh * g:(kvh + 1) * g, :]
        ck = ck_ref[:, kvh * hd:(kvh + 1) * hd]
        cv = cv_ref[:, kvh * hd:(kvh + 1) * hd]
        kn = kn_ref[:, kvh * hd:(kvh + 1) * hd]
        vn = kn_ref[:, SWA_KV_DIM + kvh * hd:SWA_KV_DIM + (kvh + 1) * hd]
        bias = bias_ref[kvh * g:(kvh + 1) * g, :]
        sink = sink_ref[kvh * g:(kvh + 1) * g, :]
        s_c = jnp.where(lane >= 1, _dot_nt(q, ck) * ATTN_SCALE + bias[:, 0:w], -jnp.inf)
        s_n = jnp.sum(q * kn, axis=-1, keepdims=True) * ATTN_SCALE + bias[:, w:w + 1]
        mx = jnp.maximum(jnp.maximum(jnp.max(s_c, axis=-1, keepdims=True), s_n), sink)
        p_c = jnp.exp(s_c - mx)
        p_n = jnp.exp(s_n - mx)
        den = jnp.sum(p_c, axis=-1, keepdims=True) + p_n + jnp.exp(sink - mx)
        o = (_dot(p_c, cv) + p_n * vn) / den
        o_ref[kvh * g:(kvh + 1) * g, :] = o


def _attn_sample(q_rows, kv_s, cache_k, cache_v, bias_s, sink_col):
    nb = kv_s.shape[0]
    w = WINDOW
    return pl.pallas_call(
        _attn_sample_kernel,
        grid=(nb,),
        in_specs=[pl.BlockSpec((SWA_HEADS, SWA_HEAD_DIM), lambda b: (b, 0)),
                  pl.BlockSpec((None, 1, 2 * SWA_KV_DIM), lambda b: (b, 0, 0)),
                  pl.BlockSpec((None, w, SWA_KV_DIM), lambda b: (b, 0, 0)),
                  pl.BlockSpec((None, w, SWA_KV_DIM), lambda b: (b, 0, 0)),
                  pl.BlockSpec((SWA_HEADS, 2 * w), lambda b: (0, 0)),
                  pl.BlockSpec((SWA_HEADS, 1), lambda b: (0, 0))],
        out_specs=pl.BlockSpec((SWA_HEADS, SWA_HEAD_DIM), lambda b: (b, 0)),
        out_shape=jax.ShapeDtypeStruct((nb * SWA_HEADS, SWA_HEAD_DIM), F32),
        compiler_params=_cparams(1),
        name="attn_sample",
    )(q_rows, kv_s.reshape(nb, 1, 2 * SWA_KV_DIM), cache_k, cache_v, bias_s, sink_col)


def kernel(x_prompt, x_sample, state_delta, state_conv, cache_k_win, cache_v_win, norm_mix, norm_ffn,
           w_in_a, conv_w_a, a_log, dt_bias, gnorm_a, w_out_a, norm_kv, w_kv, w_q_b, w_o_b, sinks,
           rel_bias, w_gate_up, w_down, norm_final):
    batch, seq, d = x_prompt.shape
    nb = x_sample.shape[0]
    xm = x_prompt.reshape(batch * seq, d)
    xs = x_sample.reshape(nb, d)
    w_ba = w_in_a[:, :, DN_QKVZ_DIM:]
    a_log = a_log.reshape(N_A_LAYERS, 1, DN_V_HEADS)
    dt_bias = dt_bias.reshape(N_A_LAYERS, 1, DN_V_HEADS)
    gnorm_a = gnorm_a.reshape(N_A_LAYERS, 1, DN_HEAD_DIM)

    def ffn(xm, xs, i):
        hm, hs = _rmsnorm_pair(xm, xs, norm_ffn[i], BF16)
        am, as_ = _linear_pair(hm, hs, w_gate_up, i, 0, D_FF, swiglu_up_col0=D_FF, out_dtype=BF16)
        return _linear_pair(am, as_, w_down, i, 0, d, res=(xm, xs))

    delta_p, conv_p, delta_s, conv_s = [], [], [], []
    for i in range(N_A_LAYERS):
        hm, hs = _rmsnorm_pair(xm, xs, norm_mix[i], BF16)
        pm, ps = _linear_pair(hm, hs, w_in_a, i, 0, DN_QKVZ_DIM)
        bam, bas = _linear_pair(hm, hs, w_ba, i, 0, 2 * DN_V_HEADS, tn=2 * DN_V_HEADS)
        om, s_p = _delta_prompt(pm, bam, conv_w_a, i, a_log, dt_bias, gnorm_a, batch, seq)
        os_, s_s = _delta_sample(ps, bas, state_conv, state_delta, conv_w_a, i, a_log, dt_bias, gnorm_a)
        xm, xs = _linear_pair(om, os_, w_out_a, i, 0, d, res=(xm, xs))
        xm, xs = ffn(xm, xs, i)
        delta_p.append(s_p)
        delta_s.append(s_s)
        conv_p.append(pm.reshape(batch, seq, DN_QKVZ_DIM)[:, seq - (DN_CONV_W - 1):, :DN_CONV_DIM])
        conv_s.append(jnp.concatenate([state_conv[i][:, 1:], ps[:, None, :DN_CONV_DIM]], axis=1))

    hm, hs = _rmsnorm_pair(xm, xs, norm_kv, BF16)
    kvm, kvs = _linear_pair(hm, hs, w_kv.reshape(1, d, 2 * SWA_KV_DIM), 0, 0, 2 * SWA_KV_DIM)
    bias = _relbias(rel_bias)
    bias_s = bias[:, 0, :]
    ck = cache_k_win.reshape(nb, WINDOW, SWA_KV_DIM)
    cv = cache_v_win.reshape(nb, WINDOW, SWA_KV_DIM)
    qd = SWA_HEADS * SWA_HEAD_DIM
    for j in range(N_B_LAYERS):
        i = N_A_LAYERS + j
        hm, hs = _rmsnorm_pair(xm, xs, norm_mix[i], BF16)
        qm, qs = _linear_pair(hm, hs, w_q_b, j, 0, qd)
        om = _attn_prompt(qm, kvm, bias, sinks[j].reshape(1, SWA_HEADS), batch, seq)
        os_ = _attn_sample(qs.reshape(nb * SWA_HEADS, SWA_HEAD_DIM), kvs, ck, cv, bias_s,
                           sinks[j].reshape(SWA_HEADS, 1))
        os_ = os_.reshape(nb, qd).astype(BF16)
        xm, xs = _linear_pair(om, os_, w_o_b, j, 0, d, res=(xm, xs))
        xm, xs = ffn(xm, xs, i)

    ym, ys = _rmsnorm_pair(xm, xs, norm_final, F32)

    kv4 = kvm.reshape(batch, seq, 2, SWA_KV_HEADS, SWA_HEAD_DIM)
    k_win_p = kv4[:, seq - WINDOW:, 0]
    v_win_p = kv4[:, seq - WINDOW:, 1]
    kvs4 = kvs.reshape(nb, 1, 2, SWA_KV_HEADS, SWA_HEAD_DIM)
    k_win_s = jnp.concatenate([cache_k_win[:, 1:], kvs4[:, :, 0]], axis=1)
    v_win_s = jnp.concatenate([cache_v_win[:, 1:], kvs4[:, :, 1]], axis=1)
    return (ym.reshape(batch, seq, d), ys.reshape(nb, 1, d),
            jnp.stack(delta_p), jnp.stack(conv_p), k_win_p, v_win_p,
            jnp.stack(delta_s), jnp.stack(conv_s), k_win_s, v_win_s)
```

```python
import functools
import math

import numpy as np
import jax
import jax.numpy as jnp
from jax import lax
from jax.experimental import pallas as pl
from jax.experimental.pallas import tpu as pltpu

F32 = jnp.float32
BF16 = jnp.bfloat16

D_MODEL = 2048
N_A_LAYERS = 2
N_B_LAYERS = 2
DN_HEAD_DIM = 128
DN_QK_HEADS = 16
DN_V_HEADS = 32
DN_KEY_DIM = DN_QK_HEADS * DN_HEAD_DIM
DN_VAL_DIM = DN_V_HEADS * DN_HEAD_DIM
DN_CONV_DIM = 2 * DN_KEY_DIM + DN_VAL_DIM
DN_CONV_W = 4
DN_CHUNK = 64
DN_QKVZ_DIM = DN_CONV_DIM + DN_VAL_DIM
SWA_HEAD_DIM = 64
SWA_HEADS = 32
SWA_KV_HEADS = 4
SWA_GROUP = SWA_HEADS // SWA_KV_HEADS
SWA_KV_DIM = SWA_KV_HEADS * SWA_HEAD_DIM
WINDOW = 128
ATTN_SCALE = SWA_HEAD_DIM ** -0.5
REL_BUCKETS = 32
REL_MAX_DIST = 128
D_FF = 5632
EPS = 1e-6

V7X_VMEM_LIMIT_BYTES = 56 * 1024 * 1024
ROW_TILE = 512
COL_TILE = 512
DELTA_PAIRS_PER_GROUP = 16


def _cparams(n_axes):
    return pltpu.CompilerParams(
        dimension_semantics=("arbitrary",) * n_axes,
        vmem_limit_bytes=V7X_VMEM_LIMIT_BYTES)


def _sigmoid(x):
    return 1.0 / (1.0 + jnp.exp(-x))


def _silu(x):
    return x * _sigmoid(x)


def _softplus(x):
    return jnp.maximum(x, 0.0) + jnp.log1p(jnp.exp(-jnp.abs(x)))


def _dot(a, b):
    return jnp.dot(a.astype(BF16), b.astype(BF16), preferred_element_type=F32)


def _dot_nt(a, b):
    return lax.dot_general(a.astype(BF16), b.astype(BF16), (((1,), (1,)), ((), ())),
                           preferred_element_type=F32)


def _dot_tn(a, b):
    return lax.dot_general(a.astype(BF16), b.astype(BF16), (((0,), (0,)), ((), ())),
                           preferred_element_type=F32)


def _split3(x):
    x1 = x.astype(BF16)
    r1 = x - x1.astype(F32)
    x2 = r1.astype(BF16)
    r2 = r1 - x2.astype(F32)
    return x1, x2, r2.astype(BF16)


def _rms(x, g):
    ms = jnp.mean(x * x, axis=-1, keepdims=True)
    return x * lax.rsqrt(ms + EPS) * g


def _rmsnorm_kernel(xm_ref, xs_ref, g_ref, om_ref, os_ref, *, n_main):
    m = pl.program_id(0)

    @pl.when(m < n_main)
    def _():
        om_ref[...] = _rms(xm_ref[...], g_ref[...]).astype(om_ref.dtype)

    @pl.when(m == n_main)
    def _():
        os_ref[...] = _rms(xs_ref[...], g_ref[...]).astype(os_ref.dtype)


def _rmsnorm_pair(xm, xs, g, out_dtype):
    mm, d = xm.shape
    ms = xs.shape[0]
    n_main = mm // ROW_TILE
    main_idx = lambda m: (jnp.minimum(m, n_main - 1), 0)
    return pl.pallas_call(
        functools.partial(_rmsnorm_kernel, n_main=n_main),
        grid=(n_main + 1,),
        in_specs=[pl.BlockSpec((ROW_TILE, d), main_idx),
                  pl.BlockSpec((ms, d), lambda m: (0, 0)),
                  pl.BlockSpec((1, d), lambda m: (0, 0))],
        out_specs=[pl.BlockSpec((ROW_TILE, d), main_idx),
                   pl.BlockSpec((ms, d), lambda m: (0, 0))],
        out_shape=[jax.ShapeDtypeStruct((mm, d), out_dtype),
                   jax.ShapeDtypeStruct((ms, d), out_dtype)],
        compiler_params=_cparams(1),
        name="rmsnorm",
    )(xm, xs, g.reshape(1, d))


def _linear_kernel(*refs, n_main, has_res, swiglu):
    it = iter(refs)
    xm_ref, xs_ref = next(it), next(it)
    w_ref = next(it)
    wu_ref = next(it) if swiglu else None
    rm_ref = next(it) if has_res else None
    rs_ref = next(it) if has_res else None
    om_ref, os_ref = next(it), next(it)
    wb_ref = next(it)
    wub_ref = next(it) if swiglu else None
    m = pl.program_id(1)

    @pl.when(m == 0)
    def _():
        wb_ref[...] = w_ref[...].astype(BF16)
        if swiglu:
            wub_ref[...] = wu_ref[...].astype(BF16)

    def compute(x_ref, r_ref, o_ref):
        x = x_ref[...]
        acc = jnp.dot(x, wb_ref[...], preferred_element_type=F32)
        if swiglu:
            acc = _silu(acc) * jnp.dot(x, wub_ref[...], preferred_element_type=F32)
        if has_res:
            acc = r_ref[...] + acc
        o_ref[...] = acc.astype(o_ref.dtype)

    @pl.when(m < n_main)
    def _():
        compute(xm_ref, rm_ref, om_ref)

    @pl.when(m == n_main)
    def _():
        compute(xs_ref, rs_ref, os_ref)


def _linear_pair(xm, xs, w3, layer, col0, n_cols, *, tn=COL_TILE, res=None,
                 swiglu_up_col0=None, out_dtype=F32):
    mm, k = xm.shape
    ms = xs.shape[0]
    n_main = mm // ROW_TILE
    nn = n_cols // tn
    assert n_cols % tn == 0 and col0 % tn == 0
    swiglu = swiglu_up_col0 is not None
    has_res = res is not None
    cb = col0 // tn
    main_x = lambda n, m: (jnp.minimum(m, n_main - 1), 0)
    main_o = lambda n, m: (jnp.minimum(m, n_main - 1), n)
    in_specs = [pl.BlockSpec((ROW_TILE, k), main_x),
                pl.BlockSpec((ms, k), lambda n, m: (0, 0)),
                pl.BlockSpec((None, k, tn), lambda n, m: (layer, 0, cb + n))]
    args = [xm, xs, w3]
    scratch = [pltpu.VMEM((k, tn), BF16)]
    if swiglu:
        ub = swiglu_up_col0 // tn
        assert swiglu_up_col0 % tn == 0
        in_specs.append(pl.BlockSpec((None, k, tn), lambda n, m: (layer, 0, ub + n)))
        args.append(w3)
        scratch.append(pltpu.VMEM((k, tn), BF16))
    if has_res:
        in_specs += [pl.BlockSpec((ROW_TILE, tn), main_o),
                     pl.BlockSpec((ms, tn), lambda n, m: (0, n))]
        args += [res[0], res[1]]
    return pl.pallas_call(
        functools.partial(_linear_kernel, n_main=n_main, has_res=has_res, swiglu=swiglu),
        grid=(nn, n_main + 1),
        in_specs=in_specs,
        out_specs=[pl.BlockSpec((ROW_TILE, tn), main_o),
                   pl.BlockSpec((ms, tn), lambda n, m: (0, n))],
        out_shape=[jax.ShapeDtypeStruct((mm, n_cols), out_dtype),
                   jax.ShapeDtypeStruct((ms, n_cols), out_dtype)],
        scratch_shapes=scratch,
        compiler_params=_cparams(2),
        name="linear",
    )(*args)


def _conv_silu_chunk(x_ref, e_ref, w_ref):
    c = DN_CHUNK
    e_ref[8:8 + c, :] = x_ref[...]
    w = w_ref[...]
    acc = e_ref[5:5 + c, :] * w[0:1]
    acc = acc + e_ref[6:6 + c, :] * w[1:2]
    acc = acc + e_ref[7:7 + c, :] * w[2:3]
    acc = acc + e_ref[8:8 + c, :] * w[3:4]
    e_ref[0:8, :] = e_ref[c:c + 8, :]
    return _silu(acc)


def _l2n(x):
    return x * lax.rsqrt(jnp.sum(x * x, axis=-1, keepdims=True) + EPS)


def _delta_prompt_kernel(q_ref, k_ref, v_ref, z_ref, ba_ref, cwq_ref, cwk_ref, cwv_ref,
                         alog_ref, dtb_ref, gn_ref,
                         o_ref, s_ref,
                         eq_ref, ek_ref, ev_ref, qs_ref, ks_ref, vs_ref, gcb_ref, bb_ref,
                         gr_ref, os_ref):
    c = DN_CHUNK
    hd = DN_HEAD_DIM
    n = pl.program_id(1)

    @pl.when(n == 0)
    def _():
        s_ref[...] = jnp.zeros_like(s_ref)
        eq_ref[0:8, :] = jnp.zeros((8, DN_KEY_DIM), F32)
        ek_ref[0:8, :] = jnp.zeros((8, DN_KEY_DIM), F32)
        ev_ref[0:8, :] = jnp.zeros((8, DN_VAL_DIM), F32)

    qc = _conv_silu_chunk(q_ref, eq_ref, cwq_ref)
    kc = _conv_silu_chunk(k_ref, ek_ref, cwk_ref)
    vc = _conv_silu_chunk(v_ref, ev_ref, cwv_ref)
    for h in range(DN_QK_HEADS):
        sl = slice(h * hd, (h + 1) * hd)
        qs_ref[h] = _l2n(qc[:, sl]) * (hd ** -0.5)
        ks_ref[h] = _l2n(kc[:, sl])
        vs_ref[h, 0:c, :] = vc[:, (2 * h) * hd:(2 * h + 1) * hd]
        vs_ref[h, c:2 * c, :] = vc[:, (2 * h + 1) * hd:(2 * h + 2) * hd]

    ba = ba_ref[...]
    beta = _sigmoid(ba[:, 0:DN_V_HEADS])
    g = -jnp.exp(alog_ref[...]) * _softplus(ba[:, DN_V_HEADS:2 * DN_V_HEADS] + dtb_ref[...])
    r64 = lax.broadcasted_iota(jnp.int32, (c, c), 0)
    c64 = lax.broadcasted_iota(jnp.int32, (c, c), 1)
    tri = (r64 >= c64).astype(BF16)
    g1, g2, g3 = _split3(g)
    gc = (jnp.dot(tri, g1, preferred_element_type=F32)
          + jnp.dot(tri, g2, preferred_element_type=F32)
          + jnp.dot(tri, g3, preferred_element_type=F32))
    pr = lax.broadcasted_iota(jnp.int32, (DN_QK_HEADS, DN_V_HEADS), 0)
    pc = lax.broadcasted_iota(jnp.int32, (DN_QK_HEADS, DN_V_HEADS), 1)
    sel_even = (pc == 2 * pr).astype(BF16)
    sel_odd = (pc == 2 * pr + 1).astype(BF16)
    nt = (((1,), (1,)), ((), ()))
    zpad = jnp.zeros((c, DN_V_HEADS), BF16)
    gr = jnp.zeros((DN_QK_HEADS, 2 * c), F32)
    for part in _split3(gc):
        gr = gr + lax.dot_general(sel_even, jnp.concatenate([part, zpad], axis=0), nt,
                                  preferred_element_type=F32)
        gr = gr + lax.dot_general(sel_odd, jnp.concatenate([zpad, part], axis=0), nt,
                                  preferred_element_type=F32)
    gr_ref[...] = gr
    for h in range(DN_V_HEADS):
        p, half = h // 2, h % 2
        gcb_ref[p, half * c:(half + 1) * c, :] = jnp.broadcast_to(gc[:, h:h + 1], (c, hd))
        bb_ref[p, half * c:(half + 1) * c, :] = jnp.broadcast_to(beta[:, h:h + 1], (c, hd))

    r2 = lax.broadcasted_iota(jnp.int32, (2 * c, 2 * c), 0)
    c2 = lax.broadcasted_iota(jnp.int32, (2 * c, 2 * c), 1)
    same = (r2 >= c) == (c2 >= c)
    incl = same & (r2 >= c2)
    strict = same & (r2 > c2)
    eye = (r2 == c2).astype(F32)

    def group_body(gi, carry):
        ps = [gi * DELTA_PAIRS_PER_GROUP + j for j in range(DELTA_PAIRS_PER_GROUP)]
        zl = lambda f, *xs: [f(*x) for x in zip(*xs)]
        q2 = [jnp.concatenate([qs_ref[p]] * 2, axis=0) for p in ps]
        k2 = [jnp.concatenate([ks_ref[p]] * 2, axis=0) for p in ps]
        v2 = [vs_ref[p] for p in ps]
        gcb = [gcb_ref[p] for p in ps]
        bb = [bb_ref[p] for p in ps]
        grow = [gr_ref[pl.ds(p, 1), :] for p in ps]
        s_a = [s_ref[2 * p] for p in ps]
        s_b = [s_ref[2 * p + 1] for p in ps]
        decay = zl(lambda g, r: jnp.where(incl, jnp.exp(jnp.where(incl, g - r, 0.0)), 0.0), gcb, grow)
        kk = zl(lambda k: _dot_nt(k, k), k2)
        a_mat = zl(lambda b, x, d: jnp.where(strict, b * x * d, 0.0), bb, kk, decay)
        t_mat = [eye - a for a in a_mat]
        pw = zl(_dot, a_mat, a_mat)
        for _ in range(4):
            t_mat = zl(lambda t, w: t + _dot(t, w), t_mat, pw)
            pw = zl(_dot, pw, pw)
        t_mat = zl(lambda t, w: t + _dot(t, w), t_mat, pw)
        egc = [jnp.exp(g) for g in gcb]
        u2 = zl(lambda t, v, b: _dot(t, v * b), t_mat, v2, bb)
        w2 = zl(lambda t, k, b, e: _dot(t, k * (b * e)), t_mat, k2, bb, egc)
        intra = zl(lambda q, k, d: _dot_nt(q, k) * d, q2, k2, decay)
        q_dec = zl(lambda q, e: q * e, q2, egc)
        gl_a = [g[c - 1:c, :] for g in gcb]
        gl_b = [g[2 * c - 1:2 * c, :] for g in gcb]
        k_dec = zl(lambda k, g, a, b: k * jnp.exp(
            jnp.concatenate([jnp.broadcast_to(a, (c, hd)), jnp.broadcast_to(b, (c, hd))], axis=0) - g),
            k2, gcb, gl_a, gl_b)
        ws = zl(lambda w, a, b: jnp.concatenate([_dot(w[0:c], a), _dot(w[c:2 * c], b)], axis=0), w2, s_a, s_b)
        v_new = zl(lambda u, x: u - x, u2, ws)
        qs = zl(lambda q, a, b: jnp.concatenate([_dot(q[0:c], a), _dot(q[c:2 * c], b)], axis=0),
                q_dec, s_a, s_b)
        o2 = zl(lambda x, i, v: x + _dot(i, v), qs, intra, v_new)
        na = zl(lambda s, g, k, v: s * jnp.exp(g) + _dot_tn(k[0:c], v[0:c]), s_a, gl_a, k_dec, v_new)
        nb_ = zl(lambda s, g, k, v: s * jnp.exp(g) + _dot_tn(k[c:2 * c], v[c:2 * c]), s_b, gl_b, k_dec, v_new)
        for j, p in enumerate(ps):
            os_ref[p] = o2[j]
            s_ref[2 * p] = na[j]
            s_ref[2 * p + 1] = nb_[j]
        return carry

    lax.fori_loop(0, DN_QK_HEADS // DELTA_PAIRS_PER_GROUP, group_body, 0)

    gn = gn_ref[...]
    for h in range(DN_V_HEADS):
        p, half = h // 2, h % 2
        o = os_ref[p, half * c:(half + 1) * c, :]
        zz = z_ref[:, h * hd:(h + 1) * hd]
        o_ref[:, h * hd:(h + 1) * hd] = (_rms(o, gn) * _silu(zz)).astype(o_ref.dtype)


def _delta_prompt(proj, ba, conv_w, layer, a_log, dt_bias, gnorm, batch, seq):
    c = DN_CHUNK
    nchunk = seq // c
    row = lambda b, n: b * nchunk + n
    kd, vd, hd = DN_KEY_DIM, DN_VAL_DIM, DN_HEAD_DIM
    in_specs = [
        pl.BlockSpec((c, kd), lambda b, n: (row(b, n), 0)),
        pl.BlockSpec((c, kd), lambda b, n: (row(b, n), 1)),
        pl.BlockSpec((c, vd), lambda b, n: (row(b, n), 1)),
        pl.BlockSpec((c, vd), lambda b, n: (row(b, n), 2)),
        pl.BlockSpec((c, 2 * DN_V_HEADS), lambda b, n: (row(b, n), 0)),
        pl.BlockSpec((None, DN_CONV_W, kd), lambda b, n: (layer, 0, 0)),
        pl.BlockSpec((None, DN_CONV_W, kd), lambda b, n: (layer, 0, 1)),
        pl.BlockSpec((None, DN_CONV_W, vd), lambda b, n: (layer, 0, 1)),
        pl.BlockSpec((None, 1, DN_V_HEADS), lambda b, n: (layer, 0, 0)),
        pl.BlockSpec((None, 1, DN_V_HEADS), lambda b, n: (layer, 0, 0)),
        pl.BlockSpec((None, 1, hd), lambda b, n: (layer, 0, 0)),
    ]
    out_specs = [
        pl.BlockSpec((c, vd), lambda b, n: (row(b, n), 0)),
        pl.BlockSpec((None, DN_V_HEADS, hd, hd), lambda b, n: (b, 0, 0, 0)),
    ]
    scratch = [
        pltpu.VMEM((c + 8, kd), F32), pltpu.VMEM((c + 8, kd), F32), pltpu.VMEM((c + 8, vd), F32),
        pltpu.VMEM((DN_QK_HEADS, c, hd), F32), pltpu.VMEM((DN_QK_HEADS, c, hd), F32),
        pltpu.VMEM((DN_QK_HEADS, 2 * c, hd), F32),
        pltpu.VMEM((DN_QK_HEADS, 2 * c, hd), F32), pltpu.VMEM((DN_QK_HEADS, 2 * c, hd), F32),
        pltpu.VMEM((DN_QK_HEADS, 2 * c), F32),
        pltpu.VMEM((DN_QK_HEADS, 2 * c, hd), F32),
    ]
    return pl.pallas_call(
        _delta_prompt_kernel,
        grid=(batch, nchunk),
        in_specs=in_specs,
        out_specs=out_specs,
        out_shape=[jax.ShapeDtypeStruct((batch * seq, vd), BF16),
                   jax.ShapeDtypeStruct((batch, DN_V_HEADS, hd, hd), F32)],
        scratch_shapes=scratch,
        compiler_params=_cparams(2),
        name="delta_prompt",
    )(proj, proj, proj, proj, ba, conv_w, conv_w, conv_w, a_log, dt_bias, gnorm)


def _delta_sample_kernel(q_ref, k_ref, v_ref, z_ref, ba_ref, pq_ref, pk_ref, pv_ref,
                         cwq_ref, cwk_ref, cwv_ref, alog_ref, dtb_ref, gn_ref, s0_ref,
                         o_ref, s_ref, kt_ref, qt_ref, osc_ref):
    hd = DN_HEAD_DIM
    nb = q_ref.shape[0]
    p = pl.program_id(0)

    def conv(prev_ref, x_ref, w_ref):
        w = w_ref[...]
        acc = prev_ref[:, 0, :] * w[0:1]
        acc = acc + prev_ref[:, 1, :] * w[1:2]
        acc = acc + prev_ref[:, 2, :] * w[2:3]
        acc = acc + x_ref[...] * w[3:4]
        return _silu(acc)

    q = _l2n(conv(pq_ref, q_ref, cwq_ref)) * (hd ** -0.5)
    k = _l2n(conv(pk_ref, k_ref, cwk_ref))
    v2 = conv(pv_ref, v_ref, cwv_ref)

    ba = ba_ref[...]
    beta_all = _sigmoid(ba[:, 0:DN_V_HEADS])
    g_all = -jnp.exp(alog_ref[...]) * _softplus(ba[:, DN_V_HEADS:2 * DN_V_HEADS] + dtb_ref[...])
    lane = lax.broadcasted_iota(jnp.int32, (nb, DN_V_HEADS), 1)

    def col(x, h):
        return jnp.sum(jnp.where(lane == h, x, 0.0), axis=1, keepdims=True)

    ri = lax.broadcasted_iota(jnp.int32, (hd, hd), 0)
    ci = lax.broadcasted_iota(jnp.int32, (hd, hd), 1)
    eye = (ri == ci).astype(BF16)

    def transpose(x):
        return sum(lax.dot_general(eye, part, (((1,), (1,)), ((), ())), preferred_element_type=F32)
                   for part in _split3(x))

    kt_ref[...] = transpose(k)
    qt_ref[...] = transpose(q)
    eg = [jnp.exp(col(g_all, 2 * p + hh)) for hh in range(2)]
    bt = [col(beta_all, 2 * p + hh) for hh in range(2)]

    for b in range(nb):
        kcb = jnp.broadcast_to(kt_ref[:, b:b + 1], (hd, hd))
        qcb = jnp.broadcast_to(qt_ref[:, b:b + 1], (hd, hd))
        for hh in range(2):
            s = s0_ref[b, hh] * eg[hh][b:b + 1, :]
            kv_mem = jnp.sum(s * kcb, axis=0, keepdims=True)
            delta = (v2[b:b + 1, hh * hd:(hh + 1) * hd] - kv_mem) * bt[hh][b:b + 1, :]
            s = s + kcb * delta
            s_ref[b, hh] = s
            osc_ref[hh, b:b + 1, :] = jnp.sum(s * qcb, axis=0, keepdims=True)

    gn = gn_ref[...]
    for hh in range(2):
        zz = z_ref[:, hh * hd:(hh + 1) * hd]
        o_ref[:, hh * hd:(hh + 1) * hd] = (_rms(osc_ref[hh], gn) * _silu(zz)).astype(o_ref.dtype)


def _delta_sample(proj_s, ba_s, state_conv, state_delta, conv_w, layer, a_log, dt_bias, gnorm):
    nb = proj_s.shape[0]
    hd = DN_HEAD_DIM
    npairs = DN_QK_HEADS
    kb = DN_KEY_DIM // hd
    vb = DN_CONV_DIM // (2 * hd) - DN_VAL_DIM // (2 * hd)
    zb = DN_CONV_DIM // (2 * hd)
    in_specs = [
        pl.BlockSpec((nb, hd), lambda p: (0, p)),
        pl.BlockSpec((nb, hd), lambda p: (0, kb + p)),
        pl.BlockSpec((nb, 2 * hd), lambda p: (0, vb + p)),
        pl.BlockSpec((nb, 2 * hd), lambda p: (0, zb + p)),
        pl.BlockSpec((nb, 2 * DN_V_HEADS), lambda p: (0, 0)),
        pl.BlockSpec((None, nb, DN_CONV_W - 1, hd), lambda p: (layer, 0, 0, p)),
        pl.BlockSpec((None, nb, DN_CONV_W - 1, hd), lambda p: (layer, 0, 0, kb + p)),
        pl.BlockSpec((None, nb, DN_CONV_W - 1, 2 * hd), lambda p: (layer, 0, 0, vb + p)),
        pl.BlockSpec((None, DN_CONV_W, hd), lambda p: (layer, 0, p)),
        pl.BlockSpec((None, DN_CONV_W, hd), lambda p: (layer, 0, kb + p)),
        pl.BlockSpec((None, DN_CONV_W, 2 * hd), lambda p: (layer, 0, vb + p)),
        pl.BlockSpec((None, 1, DN_V_HEADS), lambda p: (layer, 0, 0)),
        pl.BlockSpec((None, 1, DN_V_HEADS), lambda p: (layer, 0, 0)),
        pl.BlockSpec((None, 1, hd), lambda p: (layer, 0, 0)),
        pl.BlockSpec((None, nb, 2, hd, hd), lambda p: (layer, 0, p, 0, 0)),
    ]
    out_specs = [
        pl.BlockSpec((nb, 2 * hd), lambda p: (0, p)),
        pl.BlockSpec((nb, 2, hd, hd), lambda p: (0, p, 0, 0)),
    ]
    return pl.pallas_call(
        _delta_sample_kernel,
        grid=(npairs,),
        in_specs=in_specs,
        out_specs=out_specs,
        out_shape=[jax.ShapeDtypeStruct((nb, DN_VAL_DIM), BF16),
                   jax.ShapeDtypeStruct((nb, DN_V_HEADS, hd, hd), F32)],
        scratch_shapes=[pltpu.VMEM((hd, nb), F32), pltpu.VMEM((hd, nb), F32),
                        pltpu.VMEM((2, nb, hd), F32)],
        compiler_params=_cparams(1),
        name="delta_sample",
    )(proj_s, proj_s, proj_s, proj_s, ba_s, state_conv, state_conv, state_conv,
      conv_w, conv_w, conv_w, a_log, dt_bias, gnorm, state_delta)


def _bucket_thresholds():
    d = np.arange(0, REL_MAX_DIST, dtype=np.int64)
    max_exact = REL_BUCKETS // 2
    large = max_exact + (np.log(np.maximum(d, max_exact).astype(np.float32) / max_exact)
                         / math.log(REL_MAX_DIST / max_exact)
                         * (REL_BUCKETS - max_exact)).astype(np.int32)
    bucket = np.where(d < max_exact, d, np.minimum(large, REL_BUCKETS - 1))
    assert np.all(np.diff(bucket) >= 0)
    return [int(np.argmax(bucket >= t)) if np.any(bucket >= t) else None for t in range(REL_BUCKETS)]


def _relbias_kernel(rb_ref, o_ref):
    h = pl.program_id(0)
    qi = lax.broadcasted_iota(jnp.int32, (WINDOW, 2 * WINDOW), 0)
    ci = lax.broadcasted_iota(jnp.int32, (WINDOW, 2 * WINDOW), 1)
    dist = qi + WINDOW - ci
    val = jnp.full((WINDOW, 2 * WINDOW), rb_ref[0, h], F32)
    for t, thr in enumerate(_bucket_thresholds()):
        if t > 0 and thr is not None:
            val = jnp.where(dist >= thr, rb_ref[t, h], val)
    o_ref[...] = val


def _relbias(rel_bias):
    return pl.pallas_call(
        _relbias_kernel,
        grid=(SWA_HEADS,),
        in_specs=[pl.BlockSpec(memory_space=pltpu.SMEM)],
        out_specs=pl.BlockSpec((None, WINDOW, 2 * WINDOW), lambda h: (h, 0, 0)),
        out_shape=jax.ShapeDtypeStruct((SWA_HEADS, WINDOW, 2 * WINDOW), F32),
        compiler_params=_cparams(1),
        name="relbias",
    )(rel_bias)


def _attn_prompt_kernel(q_ref, kvc_ref, kvp_ref, bias_ref, sink_ref, o_ref):
    w = WINDOW
    hd = SWA_HEAD_DIM
    n = pl.program_id(1)
    qi = lax.broadcasted_iota(jnp.int32, (w, w), 0)
    ci = lax.broadcasted_iota(jnp.int32, (w, w), 1)
    valid_p = (ci > qi) & (n > 0)
    valid_c = ci <= qi
    neg = -jnp.inf
    for h in range(SWA_HEADS):
        kvh = h // SWA_GROUP
        q = q_ref[:, h * hd:(h + 1) * hd]
        kc = kvc_ref[:, kvh * hd:(kvh + 1) * hd]
        kp = kvp_ref[:, kvh * hd:(kvh + 1) * hd]
        vc = kvc_ref[:, SWA_KV_DIM + kvh * hd:SWA_KV_DIM + (kvh + 1) * hd]
        vp = kvp_ref[:, SWA_KV_DIM + kvh * hd:SWA_KV_DIM + (kvh + 1) * hd]
        s_p = jnp.where(valid_p, _dot_nt(q, kp) * ATTN_SCALE + bias_ref[h, :, 0:w], neg)
        s_c = jnp.where(valid_c, _dot_nt(q, kc) * ATTN_SCALE + bias_ref[h, :, w:2 * w], neg)
        sink = sink_ref[0, h]
        mx = jnp.maximum(jnp.maximum(jnp.max(s_p, axis=-1, keepdims=True),
                                     jnp.max(s_c, axis=-1, keepdims=True)), sink)
        p_p = jnp.exp(s_p - mx)
        p_c = jnp.exp(s_c - mx)
        den = (jnp.sum(p_p, axis=-1, keepdims=True) + jnp.sum(p_c, axis=-1, keepdims=True)
               + jnp.exp(sink - mx))
        o = (_dot(p_p, vp) + _dot(p_c, vc)) / den
        o_ref[:, h * hd:(h + 1) * hd] = o.astype(o_ref.dtype)


def _attn_prompt(qm, kvm, bias, sinks_row, batch, seq):
    w = WINDOW
    nb = seq // w
    qd = SWA_HEADS * SWA_HEAD_DIM
    return pl.pallas_call(
        _attn_prompt_kernel,
        grid=(batch, nb),
        in_specs=[pl.BlockSpec((w, qd), lambda b, n: (b * nb + n, 0)),
                  pl.BlockSpec((w, 2 * SWA_KV_DIM), lambda b, n: (b * nb + n, 0)),
                  pl.BlockSpec((w, 2 * SWA_KV_DIM), lambda b, n: (b * nb + jnp.maximum(n - 1, 0), 0)),
                  pl.BlockSpec((SWA_HEADS, w, 2 * w), lambda b, n: (0, 0, 0)),
                  pl.BlockSpec(memory_space=pltpu.SMEM)],
        out_specs=pl.BlockSpec((w, qd), lambda b, n: (b * nb + n, 0)),
        out_shape=jax.ShapeDtypeStruct((batch * seq, qd), BF16),
        compiler_params=_cparams(2),
        name="attn_prompt",
    )(qm, kvm, kvm, bias, sinks_row)


def _attn_sample_kernel(q_ref, kn_ref, ck_ref, cv_ref, bias_ref, sink_ref, o_ref):
    w = WINDOW
    hd = SWA_HEAD_DIM
    g = SWA_GROUP
    b = pl.program_id(0)
    lane = lax.broadcasted_iota(jnp.int32, (g, w), 1)
    for kvh in range(SWA_KV_HEADS):
        q = q_ref[kvh * g:(kvh + 1) * g, :]
        ck = ck_ref[:, kvh * hd:(kvh + 1) * hd]
        cv = cv_ref[:, kvh * hd:(kvh + 1) * hd]
        kn = kn_ref[:, kvh * hd:(kvh + 1) * hd]
        vn = kn_ref[:, SWA_KV_DIM + kvh * hd:SWA_KV_DIM + (kvh + 1) * hd]
        bias = bias_ref[kvh * g:(kvh + 1) * g, :]
        sink = sink_ref[kvh * g:(kvh + 1) * g, :]
        s_c = jnp.where(lane >= 1, _dot_nt(q, ck) * ATTN_SCALE + bias[:, 0:w], -jnp.inf)
        s_n = jnp.sum(q * kn, axis=-1, keepdims=True) * ATTN_SCALE + bias[:, w:w + 1]
        mx = jnp.maximum(jnp.maximum(jnp.max(s_c, axis=-1, keepdims=True), s_n), sink)
        p_c = jnp.exp(s_c - mx)
        p_n = jnp.exp(s_n - mx)
        den = jnp.sum(p_c, axis=-1, keepdims=True) + p_n + jnp.exp(sink - mx)
        o = (_dot(p_c, cv) + p_n * vn) / den
        o_ref[kvh * g:(kvh + 1) * g, :] = o


def _attn_sample(q_rows, kv_s, cache_k, cache_v, bias_s, sink_col):
    nb = kv_s.shape[0]
    w = WINDOW
    return pl.pallas_call(
        _attn_sample_kernel,
        grid=(nb,),
        in_specs=[pl.BlockSpec((SWA_HEADS, SWA_HEAD_DIM), lambda b: (b, 0)),
                  pl.BlockSpec((None, 1, 2 * SWA_KV_DIM), lambda b: (b, 0, 0)),
                  pl.BlockSpec((None, w, SWA_KV_DIM), lambda b: (b, 0, 0)),
                  pl.BlockSpec((None, w, SWA_KV_DIM), lambda b: (b, 0, 0)),
                  pl.BlockSpec((SWA_HEADS, 2 * w), lambda b: (0, 0)),
                  pl.BlockSpec((SWA_HEADS, 1), lambda b: (0, 0))],
        out_specs=pl.BlockSpec((SWA_HEADS, SWA_HEAD_DIM), lambda b: (b, 0)),
        out_shape=jax.ShapeDtypeStruct((nb * SWA_HEADS, SWA_HEAD_DIM), F32),
        compiler_params=_cparams(1),
        name="attn_sample",
    )(q_rows, kv_s.reshape(nb, 1, 2 * SWA_KV_DIM), cache_k, cache_v, bias_s, sink_col)


def kernel(x_prompt, x_sample, state_delta, state_conv, cache_k_win, cache_v_win, norm_mix, norm_ffn,
           w_in_a, conv_w_a, a_log, dt_bias, gnorm_a, w_out_a, norm_kv, w_kv, w_q_b, w_o_b, sinks,
           rel_bias, w_gate_up, w_down, norm_final):
    batch, seq, d = x_prompt.shape
    nb = x_sample.shape[0]
    xm = x_prompt.reshape(batch * seq, d)
    xs = x_sample.reshape(nb, d)
    w_ba = w_in_a[:, :, DN_QKVZ_DIM:]
    a_log = a_log.reshape(N_A_LAYERS, 1, DN_V_HEADS)
    dt_bias = dt_bias.reshape(N_A_LAYERS, 1, DN_V_HEADS)
    gnorm_a = gnorm_a.reshape(N_A_LAYERS, 1, DN_HEAD_DIM)

    def ffn(xm, xs, i):
        hm, hs = _rmsnorm_pair(xm, xs, norm_ffn[i], BF16)
        am, as_ = _linear_pair(hm, hs, w_gate_up, i, 0, D_FF, swiglu_up_col0=D_FF, out_dtype=BF16)
        return _linear_pair(am, as_, w_down, i, 0, d, res=(xm, xs))

    delta_p, conv_p, delta_s, conv_s = [], [], [], []
    for i in range(N_A_LAYERS):
        hm, hs = _rmsnorm_pair(xm, xs, norm_mix[i], BF16)
        pm, ps = _linear_pair(hm, hs, w_in_a, i, 0, DN_QKVZ_DIM)
        bam, bas = _linear_pair(hm, hs, w_ba, i, 0, 2 * DN_V_HEADS, tn=2 * DN_V_HEADS)
        om, s_p = _delta_prompt(pm, bam, conv_w_a, i, a_log, dt_bias, gnorm_a, batch, seq)
        os_, s_s = _delta_sample(ps, bas, state_conv, state_delta, conv_w_a, i, a_log, dt_bias, gnorm_a)
        xm, xs = _linear_pair(om, os_, w_out_a, i, 0, d, res=(xm, xs))
        xm, xs = ffn(xm, xs, i)
        delta_p.append(s_p)
        delta_s.append(s_s)
        conv_p.append(pm.reshape(batch, seq, DN_QKVZ_DIM)[:, seq - (DN_CONV_W - 1):, :DN_CONV_DIM])
        conv_s.append(jnp.concatenate([state_conv[i][:, 1:], ps[:, None, :DN_CONV_DIM]], axis=1))

    hm, hs = _rmsnorm_pair(xm, xs, norm_kv, BF16)
    kvm, kvs = _linear_pair(hm, hs, w_kv.reshape(1, d, 2 * SWA_KV_DIM), 0, 0, 2 * SWA_KV_DIM)
    bias = _relbias(rel_bias)
    bias_s = bias[:, 0, :]
    ck = cache_k_win.reshape(nb, WINDOW, SWA_KV_DIM)
    cv = cache_v_win.reshape(nb, WINDOW, SWA_KV_DIM)
    qd = SWA_HEADS * SWA_HEAD_DIM
    for j in range(N_B_LAYERS):
        i = N_A_LAYERS + j
        hm, hs = _rmsnorm_pair(xm, xs, norm_mix[i], BF16)
        qm, qs = _linear_pair(hm, hs, w_q_b, j, 0, qd)
        om = _attn_prompt(qm, kvm, bias, sinks[j].reshape(1, SWA_HEADS), batch, seq)
        os_ = _attn_sample(qs.reshape(nb * SWA_HEADS, SWA_HEAD_DIM), kvs, ck, cv, bias_s,
                           sinks[j].reshape(SWA_HEADS, 1))
        os_ = os_.reshape(nb, qd).astype(BF16)
        xm, xs = _linear_pair(om, os_, w_o_b, j, 0, d, res=(xm, xs))
        xm, xs = ffn(xm, xs, i)

    ym, ys = _rmsnorm_pair(xm, xs, norm_final, F32)

    kv4 = kvm.reshape(batch, seq, 2, SWA_KV_HEADS, SWA_HEAD_DIM)
    k_win_p = kv4[:, seq - WINDOW:, 0]
    v_win_p = kv4[:, seq - WINDOW:, 1]
    kvs4 = kvs.reshape(nb, 1, 2, SWA_KV_HEADS, SWA_HEAD_DIM)
    k_win_s = jnp.concatenate([cache_k_win[:, 1:], kvs4[:, :, 0]], axis=1)
    v_win_s = jnp.concatenate([cache_v_win[:, 1:], kvs4[:, :, 1]], axis=1)
    return (ym.reshape(batch, seq, d), ys.reshape(nb, 1, d),
            jnp.stack(delta_p), jnp.stack(conv_p), k_win_p, v_win_p,
            jnp.stack(delta_s), jnp.stack(conv_s), k_win_s, v_win_s)
```

```python
import functools
import math

import numpy as np
import jax
import jax.numpy as jnp
from jax import lax
from jax.experimental import pallas as pl
from jax.experimental.pallas import tpu as pltpu

F32 = jnp.float32
BF16 = jnp.bfloat16

D_MODEL = 2048
N_A_LAYERS = 2
N_B_LAYERS = 2
DN_HEAD_DIM = 128
DN_QK_HEADS = 16
DN_V_HEADS = 32
DN_KEY_DIM = DN_QK_HEADS * DN_HEAD_DIM
DN_VAL_DIM = DN_V_HEADS * DN_HEAD_DIM
DN_CONV_DIM = 2 * DN_KEY_DIM + DN_VAL_DIM
DN_CONV_W = 4
DN_CHUNK = 64
DN_QKVZ_DIM = DN_CONV_DIM + DN_VAL_DIM
SWA_HEAD_DIM = 64
SWA_HEADS = 32
SWA_KV_HEADS = 4
SWA_GROUP = SWA_HEADS // SWA_KV_HEADS
SWA_KV_DIM = SWA_KV_HEADS * SWA_HEAD_DIM
WINDOW = 128
ATTN_SCALE = SWA_HEAD_DIM ** -0.5
REL_BUCKETS = 32
REL_MAX_DIST = 128
D_FF = 5632
EPS = 1e-6

V7X_VMEM_LIMIT_BYTES = 56 * 1024 * 1024
ROW_TILE = 1024
COL_TILE = 512
WIDE_COL_TILE = 1024
LONG_K_ROW_TILE = 512
DELTA_PAIRS_PER_GROUP = 16


def _cparams(n_axes):
    return pltpu.CompilerParams(
        dimension_semantics=("arbitrary",) * n_axes,
        vmem_limit_bytes=V7X_VMEM_LIMIT_BYTES)


def _sigmoid(x):
    return 1.0 / (1.0 + jnp.exp(-x))


def _silu(x):
    return x * _sigmoid(x)


def _softplus(x):
    return jnp.maximum(x, 0.0) + jnp.log1p(jnp.exp(-jnp.abs(x)))


def _dot(a, b):
    return jnp.dot(a.astype(BF16), b.astype(BF16), preferred_element_type=F32)


def _dot_nt(a, b):
    return lax.dot_general(a.astype(BF16), b.astype(BF16), (((1,), (1,)), ((), ())),
                           preferred_element_type=F32)


def _dot_tn(a, b):
    return lax.dot_general(a.astype(BF16), b.astype(BF16), (((0,), (0,)), ((), ())),
                           preferred_element_type=F32)


def _split3(x):
    x1 = x.astype(BF16)
    r1 = x - x1.astype(F32)
    x2 = r1.astype(BF16)
    r2 = r1 - x2.astype(F32)
    return x1, x2, r2.astype(BF16)


def _rms(x, g):
    ms = jnp.mean(x * x, axis=-1, keepdims=True)
    return x * lax.rsqrt(ms + EPS) * g


def _rmsnorm_kernel(xm_ref, xs_ref, g_ref, om_ref, os_ref, *, n_main):
    m = pl.program_id(0)

    @pl.when(m < n_main)
    def _():
        om_ref[...] = _rms(xm_ref[...], g_ref[...]).astype(om_ref.dtype)

    @pl.when(m == n_main)
    def _():
        os_ref[...] = _rms(xs_ref[...], g_ref[...]).astype(os_ref.dtype)


def _rmsnorm_pair(xm, xs, g, out_dtype):
    mm, d = xm.shape
    ms = xs.shape[0]
    n_main = mm // ROW_TILE
    main_idx = lambda m: (jnp.minimum(m, n_main - 1), 0)
    return pl.pallas_call(
        functools.partial(_rmsnorm_kernel, n_main=n_main),
        grid=(n_main + 1,),
        in_specs=[pl.BlockSpec((ROW_TILE, d), main_idx),
                  pl.BlockSpec((ms, d), lambda m: (0, 0)),
                  pl.BlockSpec((1, d), lambda m: (0, 0))],
        out_specs=[pl.BlockSpec((ROW_TILE, d), main_idx),
                   pl.BlockSpec((ms, d), lambda m: (0, 0))],
        out_shape=[jax.ShapeDtypeStruct((mm, d), out_dtype),
                   jax.ShapeDtypeStruct((ms, d), out_dtype)],
        compiler_params=_cparams(1),
        name="rmsnorm",
    )(xm, xs, g.reshape(1, d))


def _linear_kernel(*refs, n_main, has_res, swiglu):
    it = iter(refs)
    xm_ref, xs_ref = next(it), next(it)
    w_ref = next(it)
    wu_ref = next(it) if swiglu else None
    rm_ref = next(it) if has_res else None
    rs_ref = next(it) if has_res else None
    om_ref, os_ref = next(it), next(it)
    wb_ref = next(it)
    wub_ref = next(it) if swiglu else None
    m = pl.program_id(1)

    @pl.when(m == 0)
    def _():
        wb_ref[...] = w_ref[...].astype(BF16)
        if swiglu:
            wub_ref[...] = wu_ref[...].astype(BF16)

    def compute(x_ref, r_ref, o_ref):
        x = x_ref[...]
        acc = jnp.dot(x, wb_ref[...], preferred_element_type=F32)
        if swiglu:
            acc = _silu(acc) * jnp.dot(x, wub_ref[...], preferred_element_type=F32)
        if has_res:
            acc = r_ref[...] + acc
        o_ref[...] = acc.astype(o_ref.dtype)

    @pl.when(m < n_main)
    def _():
        compute(xm_ref, rm_ref, om_ref)

    @pl.when(m == n_main)
    def _():
        compute(xs_ref, rs_ref, os_ref)


def _linear_pair(xm, xs, w3, layer, col0, n_cols, *, tm=ROW_TILE, tn=COL_TILE, res=None,
                 swiglu_up_col0=None, out_dtype=F32):
    mm, k = xm.shape
    ms = xs.shape[0]
    n_main = mm // tm
    nn = n_cols // tn
    assert n_cols % tn == 0 and col0 % tn == 0 and mm % tm == 0
    swiglu = swiglu_up_col0 is not None
    has_res = res is not None
    cb = col0 // tn
    main_x = lambda n, m: (jnp.minimum(m, n_main - 1), 0)
    main_o = lambda n, m: (jnp.minimum(m, n_main - 1), n)
    in_specs = [pl.BlockSpec((tm, k), main_x),
                pl.BlockSpec((ms, k), lambda n, m: (0, 0)),
                pl.BlockSpec((None, k, tn), lambda n, m: (layer, 0, cb + n))]
    args = [xm, xs, w3]
    scratch = [pltpu.VMEM((k, tn), BF16)]
    if swiglu:
        ub = swiglu_up_col0 // tn
        assert swiglu_up_col0 % tn == 0
        in_specs.append(pl.BlockSpec((None, k, tn), lambda n, m: (layer, 0, ub + n)))
        args.append(w3)
        scratch.append(pltpu.VMEM((k, tn), BF16))
    if has_res:
        in_specs += [pl.BlockSpec((tm, tn), main_o),
                     pl.BlockSpec((ms, tn), lambda n, m: (0, n))]
        args += [res[0], res[1]]
    return pl.pallas_call(
        functools.partial(_linear_kernel, n_main=n_main, has_res=has_res, swiglu=swiglu),
        grid=(nn, n_main + 1),
        in_specs=in_specs,
        out_specs=[pl.BlockSpec((tm, tn), main_o),
                   pl.BlockSpec((ms, tn), lambda n, m: (0, n))],
        out_shape=[jax.ShapeDtypeStruct((mm, n_cols), out_dtype),
                   jax.ShapeDtypeStruct((ms, n_cols), out_dtype)],
        scratch_shapes=scratch,
        compiler_params=_cparams(2),
        name="linear",
    )(*args)


def _conv_silu_chunk(x_ref, e_ref, w_ref):
    c = DN_CHUNK
    e_ref[8:8 + c, :] = x_ref[...]
    w = w_ref[...]
    acc = e_ref[5:5 + c, :] * w[0:1]
    acc = acc + e_ref[6:6 + c, :] * w[1:2]
    acc = acc + e_ref[7:7 + c, :] * w[2:3]
    acc = acc + e_ref[8:8 + c, :] * w[3:4]
    e_ref[0:8, :] = e_ref[c:c + 8, :]
    return _silu(acc)


def _l2n(x):
    return x * lax.rsqrt(jnp.sum(x * x, axis=-1, keepdims=True) + EPS)


def _delta_prompt_kernel(q_ref, k_ref, v_ref, z_ref, ba_ref, cwq_ref, cwk_ref, cwv_ref,
                         alog_ref, dtb_ref, gn_ref,
                         o_ref, s_ref,
                         eq_ref, ek_ref, ev_ref, qs_ref, ks_ref, vs_ref, gcb_ref, bb_ref,
                         gr_ref, os_ref):
    c = DN_CHUNK
    hd = DN_HEAD_DIM
    n = pl.program_id(1)

    @pl.when(n == 0)
    def _():
        s_ref[...] = jnp.zeros_like(s_ref)
        eq_ref[0:8, :] = jnp.zeros((8, DN_KEY_DIM), F32)
        ek_ref[0:8, :] = jnp.zeros((8, DN_KEY_DIM), F32)
        ev_ref[0:8, :] = jnp.zeros((8, DN_VAL_DIM), F32)

    qc = _conv_silu_chunk(q_ref, eq_ref, cwq_ref)
    kc = _conv_silu_chunk(k_ref, ek_ref, cwk_ref)
    vc = _conv_silu_chunk(v_ref, ev_ref, cwv_ref)
    for h in range(DN_QK_HEADS):
        sl = slice(h * hd, (h + 1) * hd)
        qs_ref[h] = _l2n(qc[:, sl]) * (hd ** -0.5)
        ks_ref[h] = _l2n(kc[:, sl])
        vs_ref[h, 0:c, :] = vc[:, (2 * h) * hd:(2 * h + 1) * hd]
        vs_ref[h, c:2 * c, :] = vc[:, (2 * h + 1) * hd:(2 * h + 2) * hd]

    ba = ba_ref[...]
    beta = _sigmoid(ba[:, 0:DN_V_HEADS])
    g = -jnp.exp(alog_ref[...]) * _softplus(ba[:, DN_V_HEADS:2 * DN_V_HEADS] + dtb_ref[...])
    r64 = lax.broadcasted_iota(jnp.int32, (c, c), 0)
    c64 = lax.broadcasted_iota(jnp.int32, (c, c), 1)
    tri = (r64 >= c64).astype(BF16)
    g1, g2, g3 = _split3(g)
    gc = (jnp.dot(tri, g1, preferred_element_type=F32)
          + jnp.dot(tri, g2, preferred_element_type=F32)
          + jnp.dot(tri, g3, preferred_element_type=F32))
    pr = lax.broadcasted_iota(jnp.int32, (DN_QK_HEADS, DN_V_HEADS), 0)
    pc = lax.broadcasted_iota(jnp.int32, (DN_QK_HEADS, DN_V_HEADS), 1)
    sel_even = (pc == 2 * pr).astype(BF16)
    sel_odd = (pc == 2 * pr + 1).astype(BF16)
    nt = (((1,), (1,)), ((), ()))
    zpad = jnp.zeros((c, DN_V_HEADS), BF16)
    gr = jnp.zeros((DN_QK_HEADS, 2 * c), F32)
    for part in _split3(gc):
        gr = gr + lax.dot_general(sel_even, jnp.concatenate([part, zpad], axis=0), nt,
                                  preferred_element_type=F32)
        gr = gr + lax.dot_general(sel_odd, jnp.concatenate([zpad, part], axis=0), nt,
                                  preferred_element_type=F32)
    gr_ref[...] = gr
    for h in range(DN_V_HEADS):
        p, half = h // 2, h % 2
        gcb_ref[p, half * c:(half + 1) * c, :] = jnp.broadcast_to(gc[:, h:h + 1], (c, hd))
        bb_ref[p, half * c:(half + 1) * c, :] = jnp.broadcast_to(beta[:, h:h + 1], (c, hd))

    r2 = lax.broadcasted_iota(jnp.int32, (2 * c, 2 * c), 0)
    c2 = lax.broadcasted_iota(jnp.int32, (2 * c, 2 * c), 1)
    same = (r2 >= c) == (c2 >= c)
    incl = same & (r2 >= c2)
    strict = same & (r2 > c2)
    eye = (r2 == c2).astype(F32)

    def group_body(gi, carry):
        ps = [gi * DELTA_PAIRS_PER_GROUP + j for j in range(DELTA_PAIRS_PER_GROUP)]
        zl = lambda f, *xs: [f(*x) for x in zip(*xs)]
        q2 = [jnp.concatenate([qs_ref[p]] * 2, axis=0) for p in ps]
        k2 = [jnp.concatenate([ks_ref[p]] * 2, axis=0) for p in ps]
        v2 = [vs_ref[p] for p in ps]
        gcb = [gcb_ref[p] for p in ps]
        bb = [bb_ref[p] for p in ps]
        grow = [gr_ref[pl.ds(p, 1), :] for p in ps]
        s_a = [s_ref[2 * p] for p in ps]
        s_b = [s_ref[2 * p + 1] for p in ps]
        decay = zl(lambda g, r: jnp.where(incl, jnp.exp(jnp.where(incl, g - r, 0.0)), 0.0), gcb, grow)
        kk = zl(lambda k: _dot_nt(k, k), k2)
        a_mat = zl(lambda b, x, d: jnp.where(strict, b * x * d, 0.0), bb, kk, decay)
        t_mat = [eye - a for a in a_mat]
        pw = zl(_dot, a_mat, a_mat)
        for _ in range(4):
            t_mat = zl(lambda t, w: t + _dot(t, w), t_mat, pw)
            pw = zl(_dot, pw, pw)
        t_mat = zl(lambda t, w: t + _dot(t, w), t_mat, pw)
        egc = [jnp.exp(g) for g in gcb]
        u2 = zl(lambda t, v, b: _dot(t, v * b), t_mat, v2, bb)
        w2 = zl(lambda t, k, b, e: _dot(t, k * (b * e)), t_mat, k2, bb, egc)
        intra = zl(lambda q, k, d: _dot_nt(q, k) * d, q2, k2, decay)
        q_dec = zl(lambda q, e: q * e, q2, egc)
        gl_a = [g[c - 1:c, :] for g in gcb]
        gl_b = [g[2 * c - 1:2 * c, :] for g in gcb]
        k_dec = zl(lambda k, g, a, b: k * jnp.exp(
            jnp.concatenate([jnp.broadcast_to(a, (c, hd)), jnp.broadcast_to(b, (c, hd))], axis=0) - g),
            k2, gcb, gl_a, gl_b)
        ws = zl(lambda w, a, b: jnp.concatenate([_dot(w[0:c], a), _dot(w[c:2 * c], b)], axis=0), w2, s_a, s_b)
        v_new = zl(lambda u, x: u - x, u2, ws)
        qs = zl(lambda q, a, b: jnp.concatenate([_dot(q[0:c], a), _dot(q[c:2 * c], b)], axis=0),
                q_dec, s_a, s_b)
        o2 = zl(lambda x, i, v: x + _dot(i, v), qs, intra, v_new)
        na = zl(lambda s, g, k, v: s * jnp.exp(g) + _dot_tn(k[0:c], v[0:c]), s_a, gl_a, k_dec, v_new)
        nb_ = zl(lambda s, g, k, v: s * jnp.exp(g) + _dot_tn(k[c:2 * c], v[c:2 * c]), s_b, gl_b, k_dec, v_new)
        for j, p in enumerate(ps):
            os_ref[p] = o2[j]
            s_ref[2 * p] = na[j]
            s_ref[2 * p + 1] = nb_[j]
        return carry

    lax.fori_loop(0, DN_QK_HEADS // DELTA_PAIRS_PER_GROUP, group_body, 0)

    gn = gn_ref[...]
    for h in range(DN_V_HEADS):
        p, half = h // 2, h % 2
        o = os_ref[p, half * c:(half + 1) * c, :]
        zz = z_ref[:, h * hd:(h + 1) * hd]
        o_ref[:, h * hd:(h + 1) * hd] = (_rms(o, gn) * _silu(zz)).astype(o_ref.dtype)


def _delta_prompt(proj, ba, conv_w, layer, a_log, dt_bias, gnorm, batch, seq, s_stack):
    c = DN_CHUNK
    nchunk = seq // c
    row = lambda b, n: b * nchunk + n
    kd, vd, hd = DN_KEY_DIM, DN_VAL_DIM, DN_HEAD_DIM
    in_specs = [
        pl.BlockSpec((c, kd), lambda b, n: (row(b, n), 0)),
        pl.BlockSpec((c, kd), lambda b, n: (row(b, n), 1)),
        pl.BlockSpec((c, vd), lambda b, n: (row(b, n), 1)),
        pl.BlockSpec((c, vd), lambda b, n: (row(b, n), 2)),
        pl.BlockSpec((c, 2 * DN_V_HEADS), lambda b, n: (row(b, n), 0)),
        pl.BlockSpec((None, DN_CONV_W, kd), lambda b, n: (layer, 0, 0)),
        pl.BlockSpec((None, DN_CONV_W, kd), lambda b, n: (layer, 0, 1)),
        pl.BlockSpec((None, DN_CONV_W, vd), lambda b, n: (layer, 0, 1)),
        pl.BlockSpec((None, 1, DN_V_HEADS), lambda b, n: (layer, 0, 0)),
        pl.BlockSpec((None, 1, DN_V_HEADS), lambda b, n: (layer, 0, 0)),
        pl.BlockSpec((None, 1, hd), lambda b, n: (layer, 0, 0)),
    ]
    out_specs = [
        pl.BlockSpec((c, vd), lambda b, n: (row(b, n), 0)),
        pl.BlockSpec((None, None, DN_V_HEADS, hd, hd), lambda b, n: (layer, b, 0, 0, 0)),
    ]
    scratch = [
        pltpu.VMEM((c + 8, kd), F32), pltpu.VMEM((c + 8, kd), F32), pltpu.VMEM((c + 8, vd), F32),
        pltpu.VMEM((DN_QK_HEADS, c, hd), F32), pltpu.VMEM((DN_QK_HEADS, c, hd), F32),
        pltpu.VMEM((DN_QK_HEADS, 2 * c, hd), F32),
        pltpu.VMEM((DN_QK_HEADS, 2 * c, hd), F32), pltpu.VMEM((DN_QK_HEADS, 2 * c, hd), F32),
        pltpu.VMEM((DN_QK_HEADS, 2 * c), F32),
        pltpu.VMEM((DN_QK_HEADS, 2 * c, hd), F32),
    ]
    args = [proj, proj, proj, proj, ba, conv_w, conv_w, conv_w, a_log, dt_bias, gnorm]
    body, aliases = _delta_prompt_kernel, {}
    if s_stack is not None:
        n_in = len(args)
        body = lambda *refs: _delta_prompt_kernel(*refs[:n_in], *refs[n_in + 1:])
        in_specs.append(pl.BlockSpec(memory_space=pl.ANY))
        args.append(s_stack)
        aliases = {n_in: 1}
    return pl.pallas_call(
        body,
        grid=(batch, nchunk),
        in_specs=in_specs,
        out_specs=out_specs,
        out_shape=[jax.ShapeDtypeStruct((batch * seq, vd), BF16),
                   jax.ShapeDtypeStruct((N_A_LAYERS, batch, DN_V_HEADS, hd, hd), F32)],
        scratch_shapes=scratch,
        input_output_aliases=aliases,
        compiler_params=_cparams(2),
        name="delta_prompt",
    )(*args)


def _delta_sample_kernel(q_ref, k_ref, v_ref, z_ref, ba_ref, pq_ref, pk_ref, pv_ref,
                         cwq_ref, cwk_ref, cwv_ref, alog_ref, dtb_ref, gn_ref, s0_ref,
                         o_ref, s_ref, kt_ref, qt_ref, osc_ref):
    hd = DN_HEAD_DIM
    nb = q_ref.shape[0]
    p = pl.program_id(0)

    def conv(prev_ref, x_ref, w_ref):
        w = w_ref[...]
        acc = prev_ref[:, 0, :] * w[0:1]
        acc = acc + prev_ref[:, 1, :] * w[1:2]
        acc = acc + prev_ref[:, 2, :] * w[2:3]
        acc = acc + x_ref[...] * w[3:4]
        return _silu(acc)

    q = _l2n(conv(pq_ref, q_ref, cwq_ref)) * (hd ** -0.5)
    k = _l2n(conv(pk_ref, k_ref, cwk_ref))
    v2 = conv(pv_ref, v_ref, cwv_ref)

    ba = ba_ref[...]
    beta_all = _sigmoid(ba[:, 0:DN_V_HEADS])
    g_all = -jnp.exp(alog_ref[...]) * _softplus(ba[:, DN_V_HEADS:2 * DN_V_HEADS] + dtb_ref[...])
    lane = lax.broadcasted_iota(jnp.int32, (nb, DN_V_HEADS), 1)

    def col(x, h):
        return jnp.sum(jnp.where(lane == h, x, 0.0), axis=1, keepdims=True)

    ri = lax.broadcasted_iota(jnp.int32, (hd, hd), 0)
    ci = lax.broadcasted_iota(jnp.int32, (hd, hd), 1)
    eye = (ri == ci).astype(BF16)

    def transpose(x):
        return sum(lax.dot_general(eye, part, (((1,), (1,)), ((), ())), preferred_element_type=F32)
                   for part in _split3(x))

    kt_ref[...] = transpose(k)
    qt_ref[...] = transpose(q)
    eg = [jnp.exp(col(g_all, 2 * p + hh)) for hh in range(2)]
    bt = [col(beta_all, 2 * p + hh) for hh in range(2)]

    for b in range(nb):
        kcb = jnp.broadcast_to(kt_ref[:, b:b + 1], (hd, hd))
        qcb = jnp.broadcast_to(qt_ref[:, b:b + 1], (hd, hd))
        for hh in range(2):
            s = s0_ref[b, hh] * eg[hh][b:b + 1, :]
            kv_mem = jnp.sum(s * kcb, axis=0, keepdims=True)
            delta = (v2[b:b + 1, hh * hd:(hh + 1) * hd] - kv_mem) * bt[hh][b:b + 1, :]
            s = s + kcb * delta
            s_ref[b, hh] = s
            osc_ref[hh, b:b + 1, :] = jnp.sum(s * qcb, axis=0, keepdims=True)

    gn = gn_ref[...]
    for hh in range(2):
        zz = z_ref[:, hh * hd:(hh + 1) * hd]
        o_ref[:, hh * hd:(hh + 1) * hd] = (_rms(osc_ref[hh], gn) * _silu(zz)).astype(o_ref.dtype)


def _delta_sample(proj_s, ba_s, state_conv, state_delta, conv_w, layer, a_log, dt_bias, gnorm, s_stack):
    nb = proj_s.shape[0]
    hd = DN_HEAD_DIM
    npairs = DN_QK_HEADS
    kb = DN_KEY_DIM // hd
    vb = DN_CONV_DIM // (2 * hd) - DN_VAL_DIM // (2 * hd)
    zb = DN_CONV_DIM // (2 * hd)
    in_specs = [
        pl.BlockSpec((nb, hd), lambda p: (0, p)),
        pl.BlockSpec((nb, hd), lambda p: (0, kb + p)),
        pl.BlockSpec((nb, 2 * hd), lambda p: (0, vb + p)),
        pl.BlockSpec((nb, 2 * hd), lambda p: (0, zb + p)),
        pl.BlockSpec((nb, 2 * DN_V_HEADS), lambda p: (0, 0)),
        pl.BlockSpec((None, nb, DN_CONV_W - 1, hd), lambda p: (layer, 0, 0, p)),
        pl.BlockSpec((None, nb, DN_CONV_W - 1, hd), lambda p: (layer, 0, 0, kb + p)),
        pl.BlockSpec((None, nb, DN_CONV_W - 1, 2 * hd), lambda p: (layer, 0, 0, vb + p)),
        pl.BlockSpec((None, DN_CONV_W, hd), lambda p: (layer, 0, p)),
        pl.BlockSpec((None, DN_CONV_W, hd), lambda p: (layer, 0, kb + p)),
        pl.BlockSpec((None, DN_CONV_W, 2 * hd), lambda p: (layer, 0, vb + p)),
        pl.BlockSpec((None, 1, DN_V_HEADS), lambda p: (layer, 0, 0)),
        pl.BlockSpec((None, 1, DN_V_HEADS), lambda p: (layer, 0, 0)),
        pl.BlockSpec((None, 1, hd), lambda p: (layer, 0, 0)),
        pl.BlockSpec((None, nb, 2, hd, hd), lambda p: (layer, 0, p, 0, 0)),
    ]
    out_specs = [
        pl.BlockSpec((nb, 2 * hd), lambda p: (0, p)),
        pl.BlockSpec((None, nb, 2, hd, hd), lambda p: (layer, 0, p, 0, 0)),
    ]
    args = [proj_s, proj_s, proj_s, proj_s, ba_s, state_conv, state_conv, state_conv,
            conv_w, conv_w, conv_w, a_log, dt_bias, gnorm, state_delta]
    body, aliases = _delta_sample_kernel, {}
    if s_stack is not None:
        n_in = len(args)
        body = lambda *refs: _delta_sample_kernel(*refs[:n_in], *refs[n_in + 1:])
        in_specs.append(pl.BlockSpec(memory_space=pl.ANY))
        args.append(s_stack)
        aliases = {n_in: 1}
    return pl.pallas_call(
        body,
        grid=(npairs,),
        in_specs=in_specs,
        out_specs=out_specs,
        out_shape=[jax.ShapeDtypeStruct((nb, DN_VAL_DIM), BF16),
                   jax.ShapeDtypeStruct((N_A_LAYERS, nb, DN_V_HEADS, hd, hd), F32)],
        scratch_shapes=[pltpu.VMEM((hd, nb), F32), pltpu.VMEM((hd, nb), F32),
                        pltpu.VMEM((2, nb, hd), F32)],
        input_output_aliases=aliases,
        compiler_params=_cparams(1),
        name="delta_sample",
    )(*args)


def _bucket_thresholds():
    d = np.arange(0, REL_MAX_DIST, dtype=np.int64)
    max_exact = REL_BUCKETS // 2
    large = max_exact + (np.log(np.maximum(d, max_exact).astype(np.float32) / max_exact)
                         / math.log(REL_MAX_DIST / max_exact)
                         * (REL_BUCKETS - max_exact)).astype(np.int32)
    bucket = np.where(d < max_exact, d, np.minimum(large, REL_BUCKETS - 1))
    assert np.all(np.diff(bucket) >= 0)
    return [int(np.argmax(bucket >= t)) if np.any(bucket >= t) else None for t in range(REL_BUCKETS)]


def _relbias_kernel(rb_ref, o_ref):
    h = pl.program_id(0)
    qi = lax.broadcasted_iota(jnp.int32, (WINDOW, 2 * WINDOW), 0)
    ci = lax.broadcasted_iota(jnp.int32, (WINDOW, 2 * WINDOW), 1)
    dist = qi + WINDOW - ci
    val = jnp.full((WINDOW, 2 * WINDOW), rb_ref[0, h], F32)
    for t, thr in enumerate(_bucket_thresholds()):
        if t > 0 and thr is not None:
            val = jnp.where(dist >= thr, rb_ref[t, h], val)
    o_ref[...] = jnp.where((dist >= 0) & (dist < WINDOW), val, -jnp.inf)


def _relbias(rel_bias):
    return pl.pallas_call(
        _relbias_kernel,
        grid=(SWA_HEADS,),
        in_specs=[pl.BlockSpec(memory_space=pltpu.SMEM)],
        out_specs=pl.BlockSpec((None, WINDOW, 2 * WINDOW), lambda h: (h, 0, 0)),
        out_shape=jax.ShapeDtypeStruct((SWA_HEADS, WINDOW, 2 * WINDOW), F32),
        compiler_params=_cparams(1),
        name="relbias",
    )(rel_bias)


def _attn_prompt_kernel(q_ref, kvc_ref, kvp_ref, bias_ref, sink_ref, o_ref):
    w = WINDOW
    hd = SWA_HEAD_DIM
    g = SWA_GROUP
    n = pl.program_id(1)
    lo = lax.broadcasted_iota(jnp.int32, (w, 2 * hd), 1) < hd

    def kv_tiles(ref, kvh):
        k = ref[:, kvh * hd:(kvh + 1) * hd].astype(BF16)
        v0 = SWA_KV_DIM + (kvh // 2) * 2 * hd
        v_pair = ref[:, v0:v0 + 2 * hd]
        v_swap = pltpu.roll(v_pair, hd, axis=1)
        v_lo, v_hi = (v_pair, v_swap) if kvh % 2 == 0 else (v_swap, v_pair)
        return k, jnp.where(lo, v_lo, 1.0).astype(BF16), jnp.where(lo, 1.0, v_hi).astype(BF16)

    def body(has_prev):
        zl = lambda f, *xs: [f(*x) for x in zip(*xs)]
        for kvh in range(SWA_KV_HEADS):
            heads = [kvh * g + j for j in range(g)]
            kc, vc_lo, vc_hi = kv_tiles(kvc_ref, kvh)
            q = [(q_ref[:, h * hd:(h + 1) * hd] * ATTN_SCALE).astype(BF16) for h in heads]
            sink = [sink_ref[0, h] for h in heads]
            s_c = [_dot_nt(x, kc) + bias_ref[h, :, w:2 * w] for x, h in zip(q, heads)]
            if has_prev:
                kp, vp_lo, vp_hi = kv_tiles(kvp_ref, kvh)
                s_p = [_dot_nt(x, kp) + bias_ref[h, :, 0:w] for x, h in zip(q, heads)]
                top = zl(jnp.maximum, s_c, s_p)
            else:
                top = s_c
            mx = zl(lambda t, sk: jnp.maximum(jnp.max(t, axis=-1, keepdims=True), sk), top, sink)
            p_c = zl(lambda s, m: jnp.exp(s - m), s_c, mx)
            o_ext = [_dot(p, vc_lo if j % 2 == 0 else vc_hi) for j, p in enumerate(p_c)]
            if has_prev:
                p_p = zl(lambda s, m: jnp.exp(s - m), s_p, mx)
                o_ext = [o + _dot(p, vp_lo if j % 2 == 0 else vp_hi)
                         for j, (o, p) in enumerate(zip(o_ext, p_p))]
            res = []
            for j in range(g):
                rowsum = o_ext[j][:, hd:hd + 1] if j % 2 == 0 else o_ext[j][:, 0:1]
                res.append(o_ext[j] / (rowsum + jnp.exp(sink[j] - mx[j])))
            for j in range(0, g, 2):
                c0 = heads[j] * hd
                o_ref[:, c0:c0 + 2 * hd] = jnp.where(lo, res[j], res[j + 1]).astype(o_ref.dtype)

    @pl.when(n > 0)
    def _():
        body(True)

    @pl.when(n == 0)
    def _():
        body(False)


def _attn_prompt(qm, kvm, bias, sinks_row, batch, seq):
    w = WINDOW
    nb = seq // w
    qd = SWA_HEADS * SWA_HEAD_DIM
    return pl.pallas_call(
        _attn_prompt_kernel,
        grid=(batch, nb),
        in_specs=[pl.BlockSpec((w, qd), lambda b, n: (b * nb + n, 0)),
                  pl.BlockSpec((w, 2 * SWA_KV_DIM), lambda b, n: (b * nb + n, 0)),
                  pl.BlockSpec((w, 2 * SWA_KV_DIM), lambda b, n: (b * nb + jnp.maximum(n - 1, 0), 0)),
                  pl.BlockSpec((SWA_HEADS, w, 2 * w), lambda b, n: (0, 0, 0)),
                  pl.BlockSpec(memory_space=pltpu.SMEM)],
        out_specs=pl.BlockSpec((w, qd), lambda b, n: (b * nb + n, 0)),
        out_shape=jax.ShapeDtypeStruct((batch * seq, qd), BF16),
        compiler_params=_cparams(2),
        name="attn_prompt",
    )(qm, kvm, kvm, bias, sinks_row)


def _attn_sample_kernel(q_ref, kn_ref, ck_ref, cv_ref, bias_ref, sink_ref, o_ref):
    w = WINDOW
    hd = SWA_HEAD_DIM
    g = SWA_GROUP
    for kvh in range(SWA_KV_HEADS):
        q = q_ref[kvh * g:(kvh + 1) * g, :]
        ck = ck_ref[:, kvh * hd:(kvh + 1) * hd]
        cv = cv_ref[:, kvh * hd:(kvh + 1) * hd]
        kn = kn_ref[:, kvh * hd:(kvh + 1) * hd]
        vn = kn_ref[:, SWA_KV_DIM + kvh * hd:SWA_KV_DIM + (kvh + 1) * hd]
        bias = bias_ref[kvh * g:(kvh + 1) * g, :]
        sink = sink_ref[kvh * g:(kvh + 1) * g, :]
        s_c = _dot_nt(q, ck) * ATTN_SCALE + bias[:, 0:w]
        s_n = jnp.sum(q * kn, axis=-1, keepdims=True) * ATTN_SCALE + bias[:, w:w + 1]
        mx = jnp.maximum(jnp.maximum(jnp.max(s_c, axis=-1, keepdims=True), s_n), sink)
        p_c = jnp.exp(s_c - mx)
        p_n = jnp.exp(s_n - mx)
        den = jnp.sum(p_c, axis=-1, keepdims=True) + p_n + jnp.exp(sink - mx)
        o = (_dot(p_c, cv) + p_n * vn) / den
        o_ref[kvh * g:(kvh + 1) * g, :] = o


def _attn_sample(q_rows, kv_s, cache_k, cache_v, bias_s, sink_col):
    nb = kv_s.shape[0]
    w = WINDOW
    return pl.pallas_call(
        _attn_sample_kernel,
        grid=(nb,),
        in_specs=[pl.BlockSpec((SWA_HEADS, SWA_HEAD_DIM), lambda b: (b, 0)),
                  pl.BlockSpec((None, 1, 2 * SWA_KV_DIM), lambda b: (b, 0, 0)),
                  pl.BlockSpec((None, w, SWA_KV_DIM), lambda b: (b, 0, 0)),
                  pl.BlockSpec((None, w, SWA_KV_DIM), lambda b: (b, 0, 0)),
                  pl.BlockSpec((SWA_HEADS, 2 * w), lambda b: (0, 0)),
                  pl.BlockSpec((SWA_HEADS, 1), lambda b: (0, 0))],
        out_specs=pl.BlockSpec((SWA_HEADS, SWA_HEAD_DIM), lambda b: (b, 0)),
        out_shape=jax.ShapeDtypeStruct((nb * SWA_HEADS, SWA_HEAD_DIM), F32),
        compiler_params=_cparams(1),
        name="attn_sample",
    )(q_rows, kv_s.reshape(nb, 1, 2 * SWA_KV_DIM), cache_k, cache_v, bias_s, sink_col)


def kernel(x_prompt, x_sample, state_delta, state_conv, cache_k_win, cache_v_win, norm_mix, norm_ffn,
           w_in_a, conv_w_a, a_log, dt_bias, gnorm_a, w_out_a, norm_kv, w_kv, w_q_b, w_o_b, sinks,
           rel_bias, w_gate_up, w_down, norm_final):
    batch, seq, d = x_prompt.shape
    nb = x_sample.shape[0]
    xm = x_prompt.reshape(batch * seq, d)
    xs = x_sample.reshape(nb, d)
    w_ba = w_in_a[:, :, DN_QKVZ_DIM:]
    a_log = a_log.reshape(N_A_LAYERS, 1, DN_V_HEADS)
    dt_bias = dt_bias.reshape(N_A_LAYERS, 1, DN_V_HEADS)
    gnorm_a = gnorm_a.reshape(N_A_LAYERS, 1, DN_HEAD_DIM)

    def ffn(xm, xs, i):
        hm, hs = _rmsnorm_pair(xm, xs, norm_ffn[i], BF16)
        am, as_ = _linear_pair(hm, hs, w_gate_up, i, 0, D_FF, swiglu_up_col0=D_FF, out_dtype=BF16)
        return _linear_pair(am, as_, w_down, i, 0, d, tm=LONG_K_ROW_TILE, res=(xm, xs))

    delta_p, delta_s, conv_p, conv_s = None, None, [], []
    for i in range(N_A_LAYERS):
        hm, hs = _rmsnorm_pair(xm, xs, norm_mix[i], BF16)
        pm, ps = _linear_pair(hm, hs, w_in_a, i, 0, DN_QKVZ_DIM, tn=WIDE_COL_TILE)
        bam, bas = _linear_pair(hm, hs, w_ba, i, 0, 2 * DN_V_HEADS, tn=2 * DN_V_HEADS)
        om, delta_p = _delta_prompt(pm, bam, conv_w_a, i, a_log, dt_bias, gnorm_a, batch, seq, delta_p)
        os_, delta_s = _delta_sample(ps, bas, state_conv, state_delta, conv_w_a, i, a_log, dt_bias,
                                     gnorm_a, delta_s)
        xm, xs = _linear_pair(om, os_, w_out_a, i, 0, d, res=(xm, xs))
        xm, xs = ffn(xm, xs, i)
        conv_p.append(pm.reshape(batch, seq, DN_QKVZ_DIM)[:, seq - (DN_CONV_W - 1):, :DN_CONV_DIM])
        conv_s.append(jnp.concatenate([state_conv[i][:, 1:], ps[:, None, :DN_CONV_DIM]], axis=1))

    hm, hs = _rmsnorm_pair(xm, xs, norm_kv, BF16)
    kvm, kvs = _linear_pair(hm, hs, w_kv.reshape(1, d, 2 * SWA_KV_DIM), 0, 0, 2 * SWA_KV_DIM)
    bias = _relbias(rel_bias)
    bias_s = bias[:, 0, :]
    ck = cache_k_win.reshape(nb, WINDOW, SWA_KV_DIM)
    cv = cache_v_win.reshape(nb, WINDOW, SWA_KV_DIM)
    qd = SWA_HEADS * SWA_HEAD_DIM
    for j in range(N_B_LAYERS):
        i = N_A_LAYERS + j
        hm, hs = _rmsnorm_pair(xm, xs, norm_mix[i], BF16)
        qm, qs = _linear_pair(hm, hs, w_q_b, j, 0, qd, tn=WIDE_COL_TILE)
        om = _attn_prompt(qm, kvm, bias, sinks[j].reshape(1, SWA_HEADS), batch, seq)
        os_ = _attn_sample(qs.reshape(nb * SWA_HEADS, SWA_HEAD_DIM), kvs, ck, cv, bias_s,
                           sinks[j].reshape(SWA_HEADS, 1))
        os_ = os_.reshape(nb, qd).astype(BF16)
        xm, xs = _linear_pair(om, os_, w_o_b, j, 0, d, res=(xm, xs))
        xm, xs = ffn(xm, xs, i)

    ym, ys = _rmsnorm_pair(xm, xs, norm_final, F32)

    kv4 = kvm.reshape(batch, seq, 2, SWA_KV_HEADS, SWA_HEAD_DIM)
    k_win_p = kv4[:, seq - WINDOW:, 0]
    v_win_p = kv4[:, seq - WINDOW:, 1]
    kvs4 = kvs.reshape(nb, 1, 2, SWA_KV_HEADS, SWA_HEAD_DIM)
    k_win_s = jnp.concatenate([cache_k_win[:, 1:], kvs4[:, :, 0]], axis=1)
    v_win_s = jnp.concatenate([cache_v_win[:, 1:], kvs4[:, :, 1]], axis=1)
    return (ym.reshape(batch, seq, d), ys.reshape(nb, 1, d),
            delta_p, jnp.stack(conv_p), k_win_p, v_win_p,
            delta_s, jnp.stack(conv_s), k_win_s, v_win_s)
```

```python
import functools
import math

import numpy as np
import jax
import jax.numpy as jnp
from jax import lax
from jax.experimental import pallas as pl
from jax.experimental.pallas import tpu as pltpu

F32 = jnp.float32
BF16 = jnp.bfloat16

D_MODEL = 2048
N_A_LAYERS = 2
N_B_LAYERS = 2
DN_HEAD_DIM = 128
DN_QK_HEADS = 16
DN_V_HEADS = 32
DN_KEY_DIM = DN_QK_HEADS * DN_HEAD_DIM
DN_VAL_DIM = DN_V_HEADS * DN_HEAD_DIM
DN_CONV_DIM = 2 * DN_KEY_DIM + DN_VAL_DIM
DN_CONV_W = 4
DN_CHUNK = 64
DN_QKVZ_DIM = DN_CONV_DIM + DN_VAL_DIM
SWA_HEAD_DIM = 64
SWA_HEADS = 32
SWA_KV_HEADS = 4
SWA_GROUP = SWA_HEADS // SWA_KV_HEADS
SWA_KV_DIM = SWA_KV_HEADS * SWA_HEAD_DIM
WINDOW = 128
ATTN_SCALE = SWA_HEAD_DIM ** -0.5
REL_BUCKETS = 32
REL_MAX_DIST = 128
D_FF = 5632
EPS = 1e-6

V7X_VMEM_LIMIT_BYTES = 56 * 1024 * 1024
ROW_TILE = 1024
COL_TILE = 512
WIDE_COL_TILE = 1024
LONG_K_ROW_TILE = 512
DELTA_PAIRS_PER_GROUP = 16


def _cparams(n_axes):
    return pltpu.CompilerParams(
        dimension_semantics=("arbitrary",) * n_axes,
        vmem_limit_bytes=V7X_VMEM_LIMIT_BYTES)


def _sigmoid(x):
    return 1.0 / (1.0 + jnp.exp(-x))


def _silu(x):
    return x * _sigmoid(x)


def _softplus(x):
    return jnp.maximum(x, 0.0) + jnp.log1p(jnp.exp(-jnp.abs(x)))


def _dot(a, b):
    return jnp.dot(a.astype(BF16), b.astype(BF16), preferred_element_type=F32)


def _dot_nt(a, b):
    return lax.dot_general(a.astype(BF16), b.astype(BF16), (((1,), (1,)), ((), ())),
                           preferred_element_type=F32)


def _dot_tn(a, b):
    return lax.dot_general(a.astype(BF16), b.astype(BF16), (((0,), (0,)), ((), ())),
                           preferred_element_type=F32)


def _split3(x):
    x1 = x.astype(BF16)
    r1 = x - x1.astype(F32)
    x2 = r1.astype(BF16)
    r2 = r1 - x2.astype(F32)
    return x1, x2, r2.astype(BF16)


def _rms(x, g):
    ms = jnp.mean(x * x, axis=-1, keepdims=True)
    return x * lax.rsqrt(ms + EPS) * g


def _rmsnorm_kernel(xm_ref, xs_ref, g_ref, om_ref, os_ref, *, n_main):
    m = pl.program_id(0)

    @pl.when(m < n_main)
    def _():
        om_ref[...] = _rms(xm_ref[...], g_ref[...]).astype(om_ref.dtype)

    @pl.when(m == n_main)
    def _():
        os_ref[...] = _rms(xs_ref[...], g_ref[...]).astype(os_ref.dtype)


def _rmsnorm_pair(xm, xs, g, out_dtype):
    mm, d = xm.shape
    ms = xs.shape[0]
    n_main = mm // ROW_TILE
    main_idx = lambda m: (jnp.minimum(m, n_main - 1), 0)
    return pl.pallas_call(
        functools.partial(_rmsnorm_kernel, n_main=n_main),
        grid=(n_main + 1,),
        in_specs=[pl.BlockSpec((ROW_TILE, d), main_idx),
                  pl.BlockSpec((ms, d), lambda m: (0, 0)),
                  pl.BlockSpec((1, d), lambda m: (0, 0))],
        out_specs=[pl.BlockSpec((ROW_TILE, d), main_idx),
                   pl.BlockSpec((ms, d), lambda m: (0, 0))],
        out_shape=[jax.ShapeDtypeStruct((mm, d), out_dtype),
                   jax.ShapeDtypeStruct((ms, d), out_dtype)],
        compiler_params=_cparams(1),
        name="rmsnorm",
    )(xm, xs, g.reshape(1, d))


def _linear_kernel(*refs, n_main, has_res, swiglu, w_t=False):
    it = iter(refs)
    xm_ref, xs_ref = next(it), next(it)
    w_ref = next(it)
    wu_ref = next(it) if swiglu else None
    rm_ref = next(it) if has_res else None
    rs_ref = next(it) if has_res else None
    om_ref, os_ref = next(it), next(it)
    wb_ref = next(it)
    wub_ref = next(it) if swiglu else None
    m = pl.program_id(1)

    @pl.when(m == 0)
    def _():
        wb_ref[...] = w_ref[...].astype(BF16)
        if swiglu:
            wub_ref[...] = wu_ref[...].astype(BF16)

    def compute(x_ref, r_ref, o_ref):
        x = x_ref[...]
        if w_t:
            acc = lax.dot_general(x, wb_ref[...], (((1,), (1,)), ((), ())), preferred_element_type=F32)
        else:
            acc = jnp.dot(x, wb_ref[...], preferred_element_type=F32)
        if swiglu:
            acc = _silu(acc) * jnp.dot(x, wub_ref[...], preferred_element_type=F32)
        if has_res:
            acc = r_ref[...] + acc
        o_ref[...] = acc.astype(o_ref.dtype)

    @pl.when(m < n_main)
    def _():
        compute(xm_ref, rm_ref, om_ref)

    @pl.when(m == n_main)
    def _():
        compute(xs_ref, rs_ref, os_ref)


def _linear_pair(xm, xs, w3, layer, col0, n_cols, *, tm=ROW_TILE, tn=COL_TILE, res=None,
                 swiglu_up_col0=None, out_dtype=F32, w_t=False):
    mm, k = xm.shape
    ms = xs.shape[0]
    n_main = mm // tm
    nn = n_cols // tn
    assert n_cols % tn == 0 and col0 % tn == 0 and mm % tm == 0
    swiglu = swiglu_up_col0 is not None
    has_res = res is not None
    cb = col0 // tn
    main_x = lambda n, m: (jnp.minimum(m, n_main - 1), 0)
    main_o = lambda n, m: (jnp.minimum(m, n_main - 1), n)
    in_specs = [pl.BlockSpec((tm, k), main_x),
                pl.BlockSpec((ms, k), lambda n, m: (0, 0)),
                pl.BlockSpec((None, k, tn), lambda n, m: (layer, 0, cb + n))]
    args = [xm, xs, w3]
    scratch = [pltpu.VMEM((k, tn), BF16)]
    if w_t:
        in_specs[2] = pl.BlockSpec((None, tn, k), lambda n, m: (layer, cb + n, 0))
        scratch = [pltpu.VMEM((tn, k), BF16)]
    if swiglu:
        ub = swiglu_up_col0 // tn
        assert swiglu_up_col0 % tn == 0
        in_specs.append(pl.BlockSpec((None, k, tn), lambda n, m: (layer, 0, ub + n)))
        args.append(w3)
        scratch.append(pltpu.VMEM((k, tn), BF16))
    if has_res:
        in_specs += [pl.BlockSpec((tm, tn), main_o),
                     pl.BlockSpec((ms, tn), lambda n, m: (0, n))]
        args += [res[0], res[1]]
    return pl.pallas_call(
        functools.partial(_linear_kernel, n_main=n_main, has_res=has_res, swiglu=swiglu, w_t=w_t),
        grid=(nn, n_main + 1),
        in_specs=in_specs,
        out_specs=[pl.BlockSpec((tm, tn), main_o),
                   pl.BlockSpec((ms, tn), lambda n, m: (0, n))],
        out_shape=[jax.ShapeDtypeStruct((mm, n_cols), out_dtype),
                   jax.ShapeDtypeStruct((ms, n_cols), out_dtype)],
        scratch_shapes=scratch,
        compiler_params=_cparams(2),
        name="linear",
    )(*args)


def _conv_silu_chunk(x_ref, e_ref, w_ref):
    c = DN_CHUNK
    e_ref[8:8 + c, :] = x_ref[...]
    w = w_ref[...]
    acc = e_ref[5:5 + c, :] * w[0:1]
    acc = acc + e_ref[6:6 + c, :] * w[1:2]
    acc = acc + e_ref[7:7 + c, :] * w[2:3]
    acc = acc + e_ref[8:8 + c, :] * w[3:4]
    e_ref[0:8, :] = e_ref[c:c + 8, :]
    return _silu(acc)


def _l2n(x):
    return x * lax.rsqrt(jnp.sum(x * x, axis=-1, keepdims=True) + EPS)


def _delta_prompt_kernel(q_ref, k_ref, v_ref, z_ref, ba_ref, cwq_ref, cwk_ref, cwv_ref,
                         alog_ref, dtb_ref, gn_ref,
                         o_ref, s_ref,
                         eq_ref, ek_ref, ev_ref, qs_ref, ks_ref, vs_ref, gcb_ref, bb_ref,
                         gr_ref, os_ref):
    c = DN_CHUNK
    hd = DN_HEAD_DIM
    n = pl.program_id(1)

    @pl.when(n == 0)
    def _():
        s_ref[...] = jnp.zeros_like(s_ref)
        eq_ref[0:8, :] = jnp.zeros((8, DN_KEY_DIM), F32)
        ek_ref[0:8, :] = jnp.zeros((8, DN_KEY_DIM), F32)
        ev_ref[0:8, :] = jnp.zeros((8, DN_VAL_DIM), F32)

    qc = _conv_silu_chunk(q_ref, eq_ref, cwq_ref)
    kc = _conv_silu_chunk(k_ref, ek_ref, cwk_ref)
    vc = _conv_silu_chunk(v_ref, ev_ref, cwv_ref)
    for h in range(DN_QK_HEADS):
        sl = slice(h * hd, (h + 1) * hd)
        qs_ref[h] = _l2n(qc[:, sl]) * (hd ** -0.5)
        ks_ref[h] = _l2n(kc[:, sl])
        vs_ref[h, 0:c, :] = vc[:, (2 * h) * hd:(2 * h + 1) * hd]
        vs_ref[h, c:2 * c, :] = vc[:, (2 * h + 1) * hd:(2 * h + 2) * hd]

    ba = ba_ref[...]
    beta = _sigmoid(ba[:, 0:DN_V_HEADS])
    g = -jnp.exp(alog_ref[...]) * _softplus(ba[:, DN_V_HEADS:2 * DN_V_HEADS] + dtb_ref[...])
    r64 = lax.broadcasted_iota(jnp.int32, (c, c), 0)
    c64 = lax.broadcasted_iota(jnp.int32, (c, c), 1)
    tri = (r64 >= c64).astype(BF16)
    g1, g2, g3 = _split3(g)
    gc = (jnp.dot(tri, g1, preferred_element_type=F32)
          + jnp.dot(tri, g2, preferred_element_type=F32)
          + jnp.dot(tri, g3, preferred_element_type=F32))
    pr = lax.broadcasted_iota(jnp.int32, (DN_QK_HEADS, DN_V_HEADS), 0)
    pc = lax.broadcasted_iota(jnp.int32, (DN_QK_HEADS, DN_V_HEADS), 1)
    sel_even = (pc == 2 * pr).astype(BF16)
    sel_odd = (pc == 2 * pr + 1).astype(BF16)
    nt = (((1,), (1,)), ((), ()))
    zpad = jnp.zeros((c, DN_V_HEADS), BF16)
    gr = jnp.zeros((DN_QK_HEADS, 2 * c), F32)
    for part in _split3(gc):
        gr = gr + lax.dot_general(sel_even, jnp.concatenate([part, zpad], axis=0), nt,
                                  preferred_element_type=F32)
        gr = gr + lax.dot_general(sel_odd, jnp.concatenate([zpad, part], axis=0), nt,
                                  preferred_element_type=F32)
    gr_ref[...] = gr
    for h in range(DN_V_HEADS):
        p, half = h // 2, h % 2
        gcb_ref[p, half * c:(half + 1) * c, :] = jnp.broadcast_to(gc[:, h:h + 1], (c, hd))
        bb_ref[p, half * c:(half + 1) * c, :] = jnp.broadcast_to(beta[:, h:h + 1], (c, hd))

    r2 = lax.broadcasted_iota(jnp.int32, (2 * c, 2 * c), 0)
    c2 = lax.broadcasted_iota(jnp.int32, (2 * c, 2 * c), 1)
    same = (r2 >= c) == (c2 >= c)
    incl = same & (r2 >= c2)
    strict = same & (r2 > c2)
    eye = (r2 == c2).astype(F32)

    def group_body(gi, carry):
        ps = [gi * DELTA_PAIRS_PER_GROUP + j for j in range(DELTA_PAIRS_PER_GROUP)]
        zl = lambda f, *xs: [f(*x) for x in zip(*xs)]
        q2 = [jnp.concatenate([qs_ref[p]] * 2, axis=0) for p in ps]
        k2 = [jnp.concatenate([ks_ref[p]] * 2, axis=0) for p in ps]
        v2 = [vs_ref[p] for p in ps]
        gcb = [gcb_ref[p] for p in ps]
        bb = [bb_ref[p] for p in ps]
        grow = [gr_ref[pl.ds(p, 1), :] for p in ps]
        s_a = [s_ref[2 * p] for p in ps]
        s_b = [s_ref[2 * p + 1] for p in ps]
        decay = zl(lambda g, r: jnp.where(incl, jnp.exp(jnp.where(incl, g - r, 0.0)), 0.0), gcb, grow)
        kk = zl(lambda k: _dot_nt(k, k), k2)
        a_mat = zl(lambda b, x, d: jnp.where(strict, b * x * d, 0.0), bb, kk, decay)
        t_mat = [eye - a for a in a_mat]
        pw = zl(_dot, a_mat, a_mat)
        for _ in range(4):
            t_mat = zl(lambda t, w: t + _dot(t, w), t_mat, pw)
            pw = zl(_dot, pw, pw)
        t_mat = zl(lambda t, w: t + _dot(t, w), t_mat, pw)
        egc = [jnp.exp(g) for g in gcb]
        u2 = zl(lambda t, v, b: _dot(t, v * b), t_mat, v2, bb)
        w2 = zl(lambda t, k, b, e: _dot(t, k * (b * e)), t_mat, k2, bb, egc)
        intra = zl(lambda q, k, d: _dot_nt(q, k) * d, q2, k2, decay)
        q_dec = zl(lambda q, e: q * e, q2, egc)
        gl_a = [g[c - 1:c, :] for g in gcb]
        gl_b = [g[2 * c - 1:2 * c, :] for g in gcb]
        k_dec = zl(lambda k, g, a, b: k * jnp.exp(
            jnp.concatenate([jnp.broadcast_to(a, (c, hd)), jnp.broadcast_to(b, (c, hd))], axis=0) - g),
            k2, gcb, gl_a, gl_b)
        ws = zl(lambda w, a, b: jnp.concatenate([_dot(w[0:c], a), _dot(w[c:2 * c], b)], axis=0), w2, s_a, s_b)
        v_new = zl(lambda u, x: u - x, u2, ws)
        qs = zl(lambda q, a, b: jnp.concatenate([_dot(q[0:c], a), _dot(q[c:2 * c], b)], axis=0),
                q_dec, s_a, s_b)
        o2 = zl(lambda x, i, v: x + _dot(i, v), qs, intra, v_new)
        na = zl(lambda s, g, k, v: s * jnp.exp(g) + _dot_tn(k[0:c], v[0:c]), s_a, gl_a, k_dec, v_new)
        nb_ = zl(lambda s, g, k, v: s * jnp.exp(g) + _dot_tn(k[c:2 * c], v[c:2 * c]), s_b, gl_b, k_dec, v_new)
        for j, p in enumerate(ps):
            os_ref[p] = o2[j]
            s_ref[2 * p] = na[j]
            s_ref[2 * p + 1] = nb_[j]
        return carry

    lax.fori_loop(0, DN_QK_HEADS // DELTA_PAIRS_PER_GROUP, group_body, 0)

    gn = gn_ref[...]
    for h in range(DN_V_HEADS):
        p, half = h // 2, h % 2
        o = os_ref[p, half * c:(half + 1) * c, :]
        zz = z_ref[:, h * hd:(h + 1) * hd]
        o_ref[:, h * hd:(h + 1) * hd] = (_rms(o, gn) * _silu(zz)).astype(o_ref.dtype)


def _delta_prompt(proj, ba, conv_w, layer, a_log, dt_bias, gnorm, batch, seq, s_stack):
    c = DN_CHUNK
    nchunk = seq // c
    row = lambda b, n: b * nchunk + n
    kd, vd, hd = DN_KEY_DIM, DN_VAL_DIM, DN_HEAD_DIM
    in_specs = [
        pl.BlockSpec((c, kd), lambda b, n: (row(b, n), 0)),
        pl.BlockSpec((c, kd), lambda b, n: (row(b, n), 1)),
        pl.BlockSpec((c, vd), lambda b, n: (row(b, n), 1)),
        pl.BlockSpec((c, vd), lambda b, n: (row(b, n), 2)),
        pl.BlockSpec((c, 2 * DN_V_HEADS), lambda b, n: (row(b, n), 0)),
        pl.BlockSpec((None, DN_CONV_W, kd), lambda b, n: (layer, 0, 0)),
        pl.BlockSpec((None, DN_CONV_W, kd), lambda b, n: (layer, 0, 1)),
        pl.BlockSpec((None, DN_CONV_W, vd), lambda b, n: (layer, 0, 1)),
        pl.BlockSpec((None, 1, DN_V_HEADS), lambda b, n: (layer, 0, 0)),
        pl.BlockSpec((None, 1, DN_V_HEADS), lambda b, n: (layer, 0, 0)),
        pl.BlockSpec((None, 1, hd), lambda b, n: (layer, 0, 0)),
    ]
    out_specs = [
        pl.BlockSpec((c, vd), lambda b, n: (row(b, n), 0)),
        pl.BlockSpec((None, None, DN_V_HEADS, hd, hd), lambda b, n: (layer, b, 0, 0, 0)),
    ]
    scratch = [
        pltpu.VMEM((c + 8, kd), F32), pltpu.VMEM((c + 8, kd), F32), pltpu.VMEM((c + 8, vd), F32),
        pltpu.VMEM((DN_QK_HEADS, c, hd), F32), pltpu.VMEM((DN_QK_HEADS, c, hd), F32),
        pltpu.VMEM((DN_QK_HEADS, 2 * c, hd), F32),
        pltpu.VMEM((DN_QK_HEADS, 2 * c, hd), F32), pltpu.VMEM((DN_QK_HEADS, 2 * c, hd), F32),
        pltpu.VMEM((DN_QK_HEADS, 2 * c), F32),
        pltpu.VMEM((DN_QK_HEADS, 2 * c, hd), F32),
    ]
    args = [proj, proj, proj, proj, ba, conv_w, conv_w, conv_w, a_log, dt_bias, gnorm]
    body, aliases = _delta_prompt_kernel, {}
    if s_stack is not None:
        n_in = len(args)
        body = lambda *refs: _delta_prompt_kernel(*refs[:n_in], *refs[n_in + 1:])
        in_specs.append(pl.BlockSpec(memory_space=pl.ANY))
        args.append(s_stack)
        aliases = {n_in: 1}
    return pl.pallas_call(
        body,
        grid=(batch, nchunk),
        in_specs=in_specs,
        out_specs=out_specs,
        out_shape=[jax.ShapeDtypeStruct((batch * seq, vd), BF16),
                   jax.ShapeDtypeStruct((N_A_LAYERS, batch, DN_V_HEADS, hd, hd), F32)],
        scratch_shapes=scratch,
        input_output_aliases=aliases,
        compiler_params=_cparams(2),
        name="delta_prompt",
    )(*args)


def _delta_sample_kernel(q_ref, k_ref, v_ref, z_ref, ba_ref, pq_ref, pk_ref, pv_ref,
                         cwq_ref, cwk_ref, cwv_ref, alog_ref, dtb_ref, gn_ref, s0_ref,
                         o_ref, s_ref, kt_ref, qt_ref, osc_ref):
    hd = DN_HEAD_DIM
    nb = q_ref.shape[0]
    p = pl.program_id(0)

    def conv(prev_ref, x_ref, w_ref):
        w = w_ref[...]
        acc = prev_ref[:, 0, :] * w[0:1]
        acc = acc + prev_ref[:, 1, :] * w[1:2]
        acc = acc + prev_ref[:, 2, :] * w[2:3]
        acc = acc + x_ref[...] * w[3:4]
        return _silu(acc)

    q = _l2n(conv(pq_ref, q_ref, cwq_ref)) * (hd ** -0.5)
    k = _l2n(conv(pk_ref, k_ref, cwk_ref))
    v2 = conv(pv_ref, v_ref, cwv_ref)

    ba = ba_ref[...]
    beta_all = _sigmoid(ba[:, 0:DN_V_HEADS])
    g_all = -jnp.exp(alog_ref[...]) * _softplus(ba[:, DN_V_HEADS:2 * DN_V_HEADS] + dtb_ref[...])
    lane = lax.broadcasted_iota(jnp.int32, (nb, DN_V_HEADS), 1)

    def col(x, h):
        return jnp.sum(jnp.where(lane == h, x, 0.0), axis=1, keepdims=True)

    ri = lax.broadcasted_iota(jnp.int32, (hd, hd), 0)
    ci = lax.broadcasted_iota(jnp.int32, (hd, hd), 1)
    eye = (ri == ci).astype(BF16)

    def transpose(x):
        return sum(lax.dot_general(eye, part, (((1,), (1,)), ((), ())), preferred_element_type=F32)
                   for part in _split3(x))

    kt_ref[...] = transpose(k)
    qt_ref[...] = transpose(q)
    eg = [jnp.exp(col(g_all, 2 * p + hh)) for hh in range(2)]
    bt = [col(beta_all, 2 * p + hh) for hh in range(2)]

    for b in range(nb):
        kcb = jnp.broadcast_to(kt_ref[:, b:b + 1], (hd, hd))
        qcb = jnp.broadcast_to(qt_ref[:, b:b + 1], (hd, hd))
        for hh in range(2):
            s = s0_ref[b, hh] * eg[hh][b:b + 1, :]
            kv_mem = jnp.sum(s * kcb, axis=0, keepdims=True)
            delta = (v2[b:b + 1, hh * hd:(hh + 1) * hd] - kv_mem) * bt[hh][b:b + 1, :]
            s = s + kcb * delta
            s_ref[b, hh] = s
            osc_ref[hh, b:b + 1, :] = jnp.sum(s * qcb, axis=0, keepdims=True)

    gn = gn_ref[...]
    for hh in range(2):
        zz = z_ref[:, hh * hd:(hh + 1) * hd]
        o_ref[:, hh * hd:(hh + 1) * hd] = (_rms(osc_ref[hh], gn) * _silu(zz)).astype(o_ref.dtype)


def _delta_sample(proj_s, ba_s, state_conv, state_delta, conv_w, layer, a_log, dt_bias, gnorm, s_stack):
    nb = proj_s.shape[0]
    hd = DN_HEAD_DIM
    npairs = DN_QK_HEADS
    kb = DN_KEY_DIM // hd
    vb = DN_CONV_DIM // (2 * hd) - DN_VAL_DIM // (2 * hd)
    zb = DN_CONV_DIM // (2 * hd)
    in_specs = [
        pl.BlockSpec((nb, hd), lambda p: (0, p)),
        pl.BlockSpec((nb, hd), lambda p: (0, kb + p)),
        pl.BlockSpec((nb, 2 * hd), lambda p: (0, vb + p)),
        pl.BlockSpec((nb, 2 * hd), lambda p: (0, zb + p)),
        pl.BlockSpec((nb, 2 * DN_V_HEADS), lambda p: (0, 0)),
        pl.BlockSpec((None, nb, DN_CONV_W - 1, hd), lambda p: (layer, 0, 0, p)),
        pl.BlockSpec((None, nb, DN_CONV_W - 1, hd), lambda p: (layer, 0, 0, kb + p)),
        pl.BlockSpec((None, nb, DN_CONV_W - 1, 2 * hd), lambda p: (layer, 0, 0, vb + p)),
        pl.BlockSpec((None, DN_CONV_W, hd), lambda p: (layer, 0, p)),
        pl.BlockSpec((None, DN_CONV_W, hd), lambda p: (layer, 0, kb + p)),
        pl.BlockSpec((None, DN_CONV_W, 2 * hd), lambda p: (layer, 0, vb + p)),
        pl.BlockSpec((None, 1, DN_V_HEADS), lambda p: (layer, 0, 0)),
        pl.BlockSpec((None, 1, DN_V_HEADS), lambda p: (layer, 0, 0)),
        pl.BlockSpec((None, 1, hd), lambda p: (layer, 0, 0)),
        pl.BlockSpec((None, nb, 2, hd, hd), lambda p: (layer, 0, p, 0, 0)),
    ]
    out_specs = [
        pl.BlockSpec((nb, 2 * hd), lambda p: (0, p)),
        pl.BlockSpec((None, nb, 2, hd, hd), lambda p: (layer, 0, p, 0, 0)),
    ]
    args = [proj_s, proj_s, proj_s, proj_s, ba_s, state_conv, state_conv, state_conv,
            conv_w, conv_w, conv_w, a_log, dt_bias, gnorm, state_delta]
    body, aliases = _delta_sample_kernel, {}
    if s_stack is not None:
        n_in = len(args)
        body = lambda *refs: _delta_sample_kernel(*refs[:n_in], *refs[n_in + 1:])
        in_specs.append(pl.BlockSpec(memory_space=pl.ANY))
        args.append(s_stack)
        aliases = {n_in: 1}
    return pl.pallas_call(
        body,
        grid=(npairs,),
        in_specs=in_specs,
        out_specs=out_specs,
        out_shape=[jax.ShapeDtypeStruct((nb, DN_VAL_DIM), BF16),
                   jax.ShapeDtypeStruct((N_A_LAYERS, nb, DN_V_HEADS, hd, hd), F32)],
        scratch_shapes=[pltpu.VMEM((hd, nb), F32), pltpu.VMEM((hd, nb), F32),
                        pltpu.VMEM((2, nb, hd), F32)],
        input_output_aliases=aliases,
        compiler_params=_cparams(1),
        name="delta_sample",
    )(*args)


def _bucket_thresholds():
    d = np.arange(0, REL_MAX_DIST, dtype=np.int64)
    max_exact = REL_BUCKETS // 2
    large = max_exact + (np.log(np.maximum(d, max_exact).astype(np.float32) / max_exact)
                         / math.log(REL_MAX_DIST / max_exact)
                         * (REL_BUCKETS - max_exact)).astype(np.int32)
    bucket = np.where(d < max_exact, d, np.minimum(large, REL_BUCKETS - 1))
    assert np.all(np.diff(bucket) >= 0)
    return [int(np.argmax(bucket >= t)) if np.any(bucket >= t) else None for t in range(REL_BUCKETS)]


def _relbias_kernel(rb_ref, o_ref):
    h = pl.program_id(0)
    qi = lax.broadcasted_iota(jnp.int32, (WINDOW, 2 * WINDOW), 0)
    ci = lax.broadcasted_iota(jnp.int32, (WINDOW, 2 * WINDOW), 1)
    dist = qi + WINDOW - ci
    val = jnp.full((WINDOW, 2 * WINDOW), rb_ref[0, h], F32)
    for t, thr in enumerate(_bucket_thresholds()):
        if t > 0 and thr is not None:
            val = jnp.where(dist >= thr, rb_ref[t, h], val)
    o_ref[...] = jnp.where((dist >= 0) & (dist < WINDOW), val, -jnp.inf)


def _relbias(rel_bias):
    return pl.pallas_call(
        _relbias_kernel,
        grid=(SWA_HEADS,),
        in_specs=[pl.BlockSpec(memory_space=pltpu.SMEM)],
        out_specs=pl.BlockSpec((None, WINDOW, 2 * WINDOW), lambda h: (h, 0, 0)),
        out_shape=jax.ShapeDtypeStruct((SWA_HEADS, WINDOW, 2 * WINDOW), F32),
        compiler_params=_cparams(1),
        name="relbias",
    )(rel_bias)


def _attn_prompt_kernel(q_ref, kvc_ref, kvp_ref, bias_ref, sink_ref, o_ref):
    w = WINDOW
    hd = SWA_HEAD_DIM
    g = SWA_GROUP
    n = pl.program_id(1)
    lo = lax.broadcasted_iota(jnp.int32, (w, 2 * hd), 1) < hd

    def kv_tiles(ref, kvh):
        k = ref[:, kvh * hd:(kvh + 1) * hd].astype(BF16)
        v0 = SWA_KV_DIM + (kvh // 2) * 2 * hd
        v_pair = ref[:, v0:v0 + 2 * hd]
        v_swap = pltpu.roll(v_pair, hd, axis=1)
        v_lo, v_hi = (v_pair, v_swap) if kvh % 2 == 0 else (v_swap, v_pair)
        return k, jnp.where(lo, v_lo, 1.0).astype(BF16), jnp.where(lo, 1.0, v_hi).astype(BF16)

    def body(has_prev):
        zl = lambda f, *xs: [f(*x) for x in zip(*xs)]
        for kvh in range(SWA_KV_HEADS):
            heads = [kvh * g + j for j in range(g)]
            kc, vc_lo, vc_hi = kv_tiles(kvc_ref, kvh)
            q = [(q_ref[:, h * hd:(h + 1) * hd] * ATTN_SCALE).astype(BF16) for h in heads]
            sink = [sink_ref[0, h] for h in heads]
            s_c = [_dot_nt(x, kc) + bias_ref[h, :, w:2 * w] for x, h in zip(q, heads)]
            if has_prev:
                kp, vp_lo, vp_hi = kv_tiles(kvp_ref, kvh)
                s_p = [_dot_nt(x, kp) + bias_ref[h, :, 0:w] for x, h in zip(q, heads)]
                top = zl(jnp.maximum, s_c, s_p)
            else:
                top = s_c
            mx = zl(lambda t, sk: jnp.maximum(jnp.max(t, axis=-1, keepdims=True), sk), top, sink)
            p_c = zl(lambda s, m: jnp.exp(s - m), s_c, mx)
            o_ext = [_dot(p, vc_lo if j % 2 == 0 else vc_hi) for j, p in enumerate(p_c)]
            if has_prev:
                p_p = zl(lambda s, m: jnp.exp(s - m), s_p, mx)
                o_ext = [o + _dot(p, vp_lo if j % 2 == 0 else vp_hi)
                         for j, (o, p) in enumerate(zip(o_ext, p_p))]
            res = []
            for j in range(g):
                rowsum = o_ext[j][:, hd:hd + 1] if j % 2 == 0 else o_ext[j][:, 0:1]
                res.append(o_ext[j] / (rowsum + jnp.exp(sink[j] - mx[j])))
            for j in range(0, g, 2):
                c0 = heads[j] * hd
                o_ref[:, c0:c0 + 2 * hd] = jnp.where(lo, res[j], res[j + 1]).astype(o_ref.dtype)

    @pl.when(n > 0)
    def _():
        body(True)

    @pl.when(n == 0)
    def _():
        body(False)


def _attn_prompt(qm, kvm, bias, sinks_row, batch, seq):
    w = WINDOW
    nb = seq // w
    qd = SWA_HEADS * SWA_HEAD_DIM
    return pl.pallas_call(
        _attn_prompt_kernel,
        grid=(batch, nb),
        in_specs=[pl.BlockSpec((w, qd), lambda b, n: (b * nb + n, 0)),
                  pl.BlockSpec((w, 2 * SWA_KV_DIM), lambda b, n: (b * nb + n, 0)),
                  pl.BlockSpec((w, 2 * SWA_KV_DIM), lambda b, n: (b * nb + jnp.maximum(n - 1, 0), 0)),
                  pl.BlockSpec((SWA_HEADS, w, 2 * w), lambda b, n: (0, 0, 0)),
                  pl.BlockSpec(memory_space=pltpu.SMEM)],
        out_specs=pl.BlockSpec((w, qd), lambda b, n: (b * nb + n, 0)),
        out_shape=jax.ShapeDtypeStruct((batch * seq, qd), BF16),
        compiler_params=_cparams(2),
        name="attn_prompt",
    )(qm, kvm, kvm, bias, sinks_row)


def _attn_sample_kernel(q_ref, kn_ref, ck_ref, cv_ref, bias_ref, sink_ref, o_ref):
    w = WINDOW
    hd = SWA_HEAD_DIM
    g = SWA_GROUP
    for kvh in range(SWA_KV_HEADS):
        q = q_ref[kvh * g:(kvh + 1) * g, :]
        ck = ck_ref[:, kvh * hd:(kvh + 1) * hd]
        cv = cv_ref[:, kvh * hd:(kvh + 1) * hd]
        kn = kn_ref[:, kvh * hd:(kvh + 1) * hd]
        vn = kn_ref[:, SWA_KV_DIM + kvh * hd:SWA_KV_DIM + (kvh + 1) * hd]
        bias = bias_ref[kvh * g:(kvh + 1) * g, :]
        sink = sink_ref[kvh * g:(kvh + 1) * g, :]
        s_c = _dot_nt(q, ck) * ATTN_SCALE + bias[:, 0:w]
        s_n = jnp.sum(q * kn, axis=-1, keepdims=True) * ATTN_SCALE + bias[:, w:w + 1]
        mx = jnp.maximum(jnp.maximum(jnp.max(s_c, axis=-1, keepdims=True), s_n), sink)
        p_c = jnp.exp(s_c - mx)
        p_n = jnp.exp(s_n - mx)
        den = jnp.sum(p_c, axis=-1, keepdims=True) + p_n + jnp.exp(sink - mx)
        o = (_dot(p_c, cv) + p_n * vn) / den
        o_ref[kvh * g:(kvh + 1) * g, :] = o


def _attn_sample(q_rows, kv_s, cache_k, cache_v, bias_s, sink_col):
    nb = kv_s.shape[0]
    w = WINDOW
    return pl.pallas_call(
        _attn_sample_kernel,
        grid=(nb,),
        in_specs=[pl.BlockSpec((SWA_HEADS, SWA_HEAD_DIM), lambda b: (b, 0)),
                  pl.BlockSpec((None, 1, 2 * SWA_KV_DIM), lambda b: (b, 0, 0)),
                  pl.BlockSpec((None, w, SWA_KV_DIM), lambda b: (b, 0, 0)),
                  pl.BlockSpec((None, w, SWA_KV_DIM), lambda b: (b, 0, 0)),
                  pl.BlockSpec((SWA_HEADS, 2 * w), lambda b: (0, 0)),
                  pl.BlockSpec((SWA_HEADS, 1), lambda b: (0, 0))],
        out_specs=pl.BlockSpec((SWA_HEADS, SWA_HEAD_DIM), lambda b: (b, 0)),
        out_shape=jax.ShapeDtypeStruct((nb * SWA_HEADS, SWA_HEAD_DIM), F32),
        compiler_params=_cparams(1),
        name="attn_sample",
    )(q_rows, kv_s.reshape(nb, 1, 2 * SWA_KV_DIM), cache_k, cache_v, bias_s, sink_col)


def kernel(x_prompt, x_sample, state_delta, state_conv, cache_k_win, cache_v_win, norm_mix, norm_ffn,
           w_in_a, conv_w_a, a_log, dt_bias, gnorm_a, w_out_a, norm_kv, w_kv, w_q_b, w_o_b, sinks,
           rel_bias, w_gate_up, w_down, norm_final):
    batch, seq, d = x_prompt.shape
    nb = x_sample.shape[0]
    xm = x_prompt.reshape(batch * seq, d)
    xs = x_sample.reshape(nb, d)
    w_in_t = jnp.swapaxes(w_in_a, 1, 2)
    a_log = a_log.reshape(N_A_LAYERS, 1, DN_V_HEADS)
    dt_bias = dt_bias.reshape(N_A_LAYERS, 1, DN_V_HEADS)
    gnorm_a = gnorm_a.reshape(N_A_LAYERS, 1, DN_HEAD_DIM)

    def ffn(xm, xs, i):
        hm, hs = _rmsnorm_pair(xm, xs, norm_ffn[i], BF16)
        am, as_ = _linear_pair(hm, hs, w_gate_up, i, 0, D_FF, swiglu_up_col0=D_FF, out_dtype=BF16)
        return _linear_pair(am, as_, w_down, i, 0, d, tm=LONG_K_ROW_TILE, res=(xm, xs))

    delta_p, delta_s, conv_p, conv_s = None, None, [], []
    for i in range(N_A_LAYERS):
        hm, hs = _rmsnorm_pair(xm, xs, norm_mix[i], BF16)
        pm, ps = _linear_pair(hm, hs, w_in_t, i, 0, DN_QKVZ_DIM, tn=WIDE_COL_TILE, w_t=True)
        bam, bas = _linear_pair(hm, hs, w_in_t, i, DN_QKVZ_DIM, 2 * DN_V_HEADS, tn=2 * DN_V_HEADS, w_t=True)
        om, delta_p = _delta_prompt(pm, bam, conv_w_a, i, a_log, dt_bias, gnorm_a, batch, seq, delta_p)
        os_, delta_s = _delta_sample(ps, bas, state_conv, state_delta, conv_w_a, i, a_log, dt_bias,
                                     gnorm_a, delta_s)
        xm, xs = _linear_pair(om, os_, w_out_a, i, 0, d, res=(xm, xs))
        xm, xs = ffn(xm, xs, i)
        conv_p.append(pm.reshape(batch, seq, DN_QKVZ_DIM)[:, seq - (DN_CONV_W - 1):, :DN_CONV_DIM])
        conv_s.append(jnp.concatenate([state_conv[i][:, 1:], ps[:, None, :DN_CONV_DIM]], axis=1))

    hm, hs = _rmsnorm_pair(xm, xs, norm_kv, BF16)
    kvm, kvs = _linear_pair(hm, hs, w_kv.reshape(1, d, 2 * SWA_KV_DIM), 0, 0, 2 * SWA_KV_DIM)
    bias = _relbias(rel_bias)
    bias_s = bias[:, 0, :]
    ck = cache_k_win.reshape(nb, WINDOW, SWA_KV_DIM)
    cv = cache_v_win.reshape(nb, WINDOW, SWA_KV_DIM)
    qd = SWA_HEADS * SWA_HEAD_DIM
    for j in range(N_B_LAYERS):
        i = N_A_LAYERS + j
        hm, hs = _rmsnorm_pair(xm, xs, norm_mix[i], BF16)
        qm, qs = _linear_pair(hm, hs, w_q_b, j, 0, qd, tn=WIDE_COL_TILE)
        om = _attn_prompt(qm, kvm, bias, sinks[j].reshape(1, SWA_HEADS), batch, seq)
        os_ = _attn_sample(qs.reshape(nb * SWA_HEADS, SWA_HEAD_DIM), kvs, ck, cv, bias_s,
                           sinks[j].reshape(SWA_HEADS, 1))
        os_ = os_.reshape(nb, qd).astype(BF16)
        xm, xs = _linear_pair(om, os_, w_o_b, j, 0, d, res=(xm, xs))
        xm, xs = ffn(xm, xs, i)

    ym, ys = _rmsnorm_pair(xm, xs, norm_final, F32)

    kv4 = kvm.reshape(batch, seq, 2, SWA_KV_HEADS, SWA_HEAD_DIM)
    k_win_p = kv4[:, seq - WINDOW:, 0]
    v_win_p = kv4[:, seq - WINDOW:, 1]
    kvs4 = kvs.reshape(nb, 1, 2, SWA_KV_HEADS, SWA_HEAD_DIM)
    k_win_s = jnp.concatenate([cache_k_win[:, 1:], kvs4[:, :, 0]], axis=1)
    v_win_s = jnp.concatenate([cache_v_win[:, 1:], kvs4[:, :, 1]], axis=1)
    return (ym.reshape(batch, seq, d), ys.reshape(nb, 1, d),
            delta_p, jnp.stack(conv_p), k_win_p, v_win_p,
            delta_s, jnp.stack(conv_s), k_win_s, v_win_s)
```
